```python
import math
import jax, jax.numpy as jnp
from jax import lax
import numpy as np

D_MODEL = 2048
BATCH = 2
SEQ = 4096
DEPTH = 1

D_MIX = D_MODEL
HEAD_DIM = 128
ATTN_WIDTH = D_MIX // 2
ATTN_HEADS = ATTN_WIDTH // HEAD_DIM
SSM_WIDTH = D_MIX - ATTN_WIDTH
SSM_GROUP = 16
SSM_GROUPS = SSM_WIDTH // SSM_GROUP
SSM_STATE = 64
DILATED_PATTERNS = ((128, 1), (512, 4), (2048, 16))
BLK = 128
D_FF = 4 * D_MODEL
PLE_DIM = 256
RMS_EPS = 1e-6
NEG_INF = -1e30

kernel_name = "hybrid_dilated_attn_s5_block"


def _rmsnorm(x, g):
    xf = x.astype(jnp.float32)
    y = xf * lax.rsqrt(jnp.mean(xf * xf, axis=-1, keepdims=True) + RMS_EPS)
    return (y * g.astype(jnp.float32)).astype(x.dtype)


def _dilated_branch(q, k, v, window, dilation):
    B, S, H, E = q.shape
    M = S // dilation
    n_keys = window // dilation
    nb = -(-M // BLK)
    Mp = nb * BLK

    def blocks(t):
        t = t.reshape(B, M, dilation, H, E)
        t = jnp.pad(t, ((0, 0), (0, Mp - M), (0, 0), (0, 0), (0, 0)))
        return t.reshape(B, nb, BLK, dilation, H, E)

    def with_prev(t):
        prev = jnp.pad(t[:, :-1], ((0, 0), (1, 0), (0, 0), (0, 0), (0, 0), (0, 0)))
        return jnp.concatenate([prev, t], axis=2)

    qb = blocks(q)
    kk = with_prev(blocks(k))
    vv = with_prev(blocks(v))

    scale = 1.0 / math.sqrt(E)
    s = jnp.einsum('bnqrhe,bnkrhe->bnrhqk', qb, kk,
                   preferred_element_type=jnp.float32) * scale
    i = jnp.arange(BLK)
    j = jnp.arange(2 * BLK)
    blk = jnp.arange(nb)
    dist = BLK + i[:, None] - j[None, :]
    kpos = (blk[:, None] - 1) * BLK + j[None, :]
    mask = ((dist >= 0) & (dist <= n_keys))[None] & (kpos >= 0)[:, None, :]
    s = jnp.where(mask[None, :, None, None], s, NEG_INF)
    lse = jax.nn.logsumexp(s, axis=-1)
    prob = jnp.exp(s - lse[..., None])
    o = jnp.einsum('bnrhqk,bnkrhe->bnqrhe', prob, vv.astype(jnp.float32))
    o = o.reshape(B, Mp, dilation, H, E)[:, :M].reshape(B, S, H, E)
    lse = lse.transpose(0, 1, 4, 2, 3).reshape(B, Mp, dilation, H)[:, :M].reshape(B, S, H)
    return o, lse


def _dilated_attention(q, k, v):
    outs, lses = [], []
    for window, dilation in DILATED_PATTERNS:
        o, l = _dilated_branch(q, k, v, window, dilation)
        outs.append(o)
        lses.append(l)
    w = jax.nn.softmax(jnp.stack(lses, axis=0), axis=0)
    o = jnp.sum(w[..., None] * jnp.stack(outs, axis=0), axis=0)
    B, S, H, E = q.shape
    return o.reshape(B, S, H * E).astype(q.dtype)


def _ssm_combine(e1, e2):
    a1r, a1i, b1r, b1i = e1
    a2r, a2i, b2r, b2i = e2
    ar = a1r * a2r - a1i * a2i
    ai = a1r * a2i + a1i * a2r
    br = a2r * b1r - a2i * b1i + b2r
    bi = a2r * b1i + a2i * b1r + b2i
    return (ar, ai, br, bi)


def _s5(u, lam_re, lam_im, log_dt, b_re, b_im, c_re, c_im, d_skip, w_glu, b_glu):
    B, S, _ = u.shape
    uf = u.astype(jnp.float32).reshape(B, S, SSM_GROUPS, SSM_GROUP)
    lr0 = lam_re.astype(jnp.float32)
    li0 = lam_im.astype(jnp.float32)
    dt = jnp.exp(log_dt.astype(jnp.float32))[:, None]
    mag = jnp.exp(lr0 * dt)
    abar_r = mag * jnp.cos(li0 * dt)
    abar_i = mag * jnp.sin(li0 * dt)
    num_r = abar_r - 1.0
    num_i = abar_i
    den = lr0 * lr0 + li0 * li0
    coef_r = ((num_r * lr0 + num_i * li0) / den)[..., None]
    coef_i = ((num_i * lr0 - num_r * li0) / den)[..., None]
    br = b_re.astype(jnp.float32)
    bi = b_im.astype(jnp.float32)
    bbar_r = coef_r * br - coef_i * bi
    bbar_i = coef_r * bi + coef_i * br
    bu_r = jnp.einsum('bsgc,gpc->bsgp', uf, bbar_r)
    bu_i = jnp.einsum('bsgc,gpc->bsgp', uf, bbar_i)
    a_r = jnp.broadcast_to(abar_r[None, None], (1, S, SSM_GROUPS, SSM_STATE))
    a_i = jnp.broadcast_to(abar_i[None, None], (1, S, SSM_GROUPS, SSM_STATE))
    _, _, st_r, st_i = lax.associative_scan(_ssm_combine, (a_r, a_i, bu_r, bu_i), axis=1)
    y = (jnp.einsum('gcp,bsgp->bsgc', c_re.astype(jnp.float32), st_r)
         - jnp.einsum('gcp,bsgp->bsgc', c_im.astype(jnp.float32), st_i))
    y = y.reshape(B, S, SSM_WIDTH) + d_skip.astype(jnp.float32) * u.astype(jnp.float32)
    y = jax.nn.gelu(y).astype(u.dtype)
    gate = jax.nn.sigmoid(y @ w_glu + b_glu)
    return y * gate


def setup_inputs(seed: int = 0) -> dict:
    key = jax.random.key(seed)
    ks = jax.random.split(key, 32)
    f32 = jnp.float32
    L = DEPTH

    def nrm(k, shape, scale):
        return jax.random.normal(k, shape, f32) * scale

    def gain(k, shape):
        return 1.0 + 0.02 * jax.random.normal(k, shape, f32)

    G, P, C = SSM_GROUPS, SSM_STATE, SSM_GROUP
    n_idx = jnp.arange(P, dtype=f32)
    return {
        "x": nrm(ks[0], (BATCH, SEQ, D_MODEL), 1.0),
        "p": nrm(ks[1], (DEPTH, BATCH, SEQ, PLE_DIM), 1.0),
        "mix_norm_pre": gain(ks[2], (L, D_MODEL)),
        "w_in": nrm(ks[3], (L, D_MODEL, 3 * ATTN_WIDTH + SSM_WIDTH), D_MODEL ** -0.5),
        "lam_re": -0.5 + 0.01 * jax.random.normal(ks[4], (L, G, P), f32),
        "lam_im": math.pi * n_idx + 0.01 * jax.random.normal(ks[5], (L, G, P), f32),
        "log_dt": jax.random.uniform(ks[6], (L, G), f32, math.log(1e-3), math.log(1e-1)),
        "ssm_b_re": nrm(ks[7], (L, G, P, C), (2 * C) ** -0.5),
        "ssm_b_im": nrm(ks[8], (L, G, P, C), (2 * C) ** -0.5),
        "ssm_c_re": nrm(ks[9], (L, G, C, P), (2 * P) ** -0.5),
        "ssm_c_im": nrm(ks[10], (L, G, C, P), (2 * P) ** -0.5),
        "ssm_d": nrm(ks[11], (L, SSM_WIDTH), 1.0),
        "w_glu": nrm(ks[12], (L, SSM_WIDTH, SSM_WIDTH), SSM_WIDTH ** -0.5),
        "b_glu": nrm(ks[13], (L, SSM_WIDTH), 0.01),
        "attn_out_norm": gain(ks[14], (L, ATTN_WIDTH)),
        "ssm_out_norm": gain(ks[15], (L, SSM_WIDTH)),
        "w_out": nrm(ks[16], (L, D_MIX, D_MODEL), D_MIX ** -0.5),
        "mix_norm_post": gain(ks[17], (L, D_MODEL)),
        "mlp_norm_pre": gain(ks[18], (L, D_MODEL)),
        "w_up": nrm(ks[19], (L, D_MODEL, D_FF), D_MODEL ** -0.5),
        "w_down": nrm(ks[20], (L, D_FF, D_MODEL), D_FF ** -0.5),
        "mlp_norm_post": gain(ks[21], (L, D_MODEL)),
        "ple_norm_pre": gain(ks[22], (L, D_MODEL)),
        "w_ple_gate": nrm(ks[23], (L, D_MODEL, D_MODEL), D_MODEL ** -0.5),
        "w_ple_proj": nrm(ks[24], (L, PLE_DIM, D_MODEL), PLE_DIM ** -0.5),
        "ple_norm_post": gain(ks[25], (L, D_MODEL)),
    }


def reference(x, p, mix_norm_pre, w_in, lam_re, lam_im, log_dt, ssm_b_re, ssm_b_im,
              ssm_c_re, ssm_c_im, ssm_d, w_glu, b_glu, attn_out_norm, ssm_out_norm,
              w_out, mix_norm_post, mlp_norm_pre, w_up, w_down, mlp_norm_post,
              ple_norm_pre, w_ple_gate, w_ple_proj, ple_norm_post):
    B, S, _ = x.shape
    h = x
    for i in range(DEPTH):
        hn = _rmsnorm(h, mix_norm_pre[i])
        proj = hn @ w_in[i]
        q = proj[..., :ATTN_WIDTH].reshape(B, S, ATTN_HEADS, HEAD_DIM)
        k = proj[..., ATTN_WIDTH:2 * ATTN_WIDTH].reshape(B, S, ATTN_HEADS, HEAD_DIM)
        v = proj[..., 2 * ATTN_WIDTH:3 * ATTN_WIDTH].reshape(B, S, ATTN_HEADS, HEAD_DIM)
        u = proj[..., 3 * ATTN_WIDTH:]
        attn = _dilated_attention(q, k, v)
        ssm = _s5(u, lam_re[i], lam_im[i], log_dt[i], ssm_b_re[i], ssm_b_im[i],
                  ssm_c_re[i], ssm_c_im[i], ssm_d[i], w_glu[i], b_glu[i])
        mixed = jnp.concatenate([_rmsnorm(attn, attn_out_norm[i]),
                                 _rmsnorm(ssm, ssm_out_norm[i])], axis=-1)
        h = h + _rmsnorm(mixed @ w_out[i], mix_norm_post[i])
        hn = _rmsnorm(h, mlp_norm_pre[i])
        ff = jnp.square(jax.nn.relu(hn @ w_up[i])) @ w_down[i]
        h = h + _rmsnorm(ff, mlp_norm_post[i])
        gate = jax.nn.sigmoid(_rmsnorm(h, ple_norm_pre[i]) @ w_ple_gate[i])
        e = p[i] @ w_ple_proj[i]
        h = h + _rmsnorm(gate * e, ple_norm_post[i])
    return h
```

```python
import functools
import math

import numpy as np
import jax
import jax.numpy as jnp
from jax import lax
from jax.experimental import pallas as pl
from jax.experimental.pallas import tpu as pltpu

F32 = jnp.float32
BF16 = jnp.bfloat16

D_MODEL = 2048
HEAD_DIM = 128
ATTN_WIDTH = 1024
ATTN_HEADS = ATTN_WIDTH // HEAD_DIM
SSM_WIDTH = 1024
SSM_GROUP = 16
SSM_GROUPS = SSM_WIDTH // SSM_GROUP
SSM_STATE = 64
DILATIONS = (1, 4, 16)
BLK = 128
D_FF = 4 * D_MODEL
PLE_DIM = 256
RMS_EPS = 1e-6
NEG_INF = -1e30

V7X_VMEM_BYTES = 64 * 1024 * 1024
VMEM_LIMIT = 52 * 1024 * 1024

SSM_TILE = 512
SSM_CHUNKS = 8
SSM_CHUNK_LEN = SSM_TILE // SSM_CHUNKS
SSM_LANE_BLOCK = 128
SSM_SLABS = SSM_WIDTH // SSM_LANE_BLOCK
SLAB_STATES = (SSM_LANE_BLOCK // SSM_GROUP) * SSM_STATE


def _rms(x, g):
    ms = jnp.mean(x * x, axis=-1, keepdims=True)
    return x * lax.rsqrt(ms + RMS_EPS) * g


def _params(*sem):
    return pltpu.CompilerParams(dimension_semantics=sem, vmem_limit_bytes=VMEM_LIMIT)


def _in_proj_kernel(x_ref, g_ref, w_ref, o_ref, xn_ref):
    @pl.when(pl.program_id(1) == 0)
    def _():
        xn_ref[...] = _rms(x_ref[...], g_ref[...]).astype(BF16)

    o_ref[...] = jnp.dot(xn_ref[...], w_ref[...], preferred_element_type=F32).astype(BF16)


def _in_proj(x2, g, w_bf16, tm=1024, tn=1024):
    T, D = x2.shape
    N = w_bf16.shape[1]
    return pl.pallas_call(
        _in_proj_kernel,
        grid=(T // tm, N // tn),
        in_specs=[
            pl.BlockSpec((tm, D), lambda i, j: (i, 0)),
            pl.BlockSpec((1, D), lambda i, j: (0, 0)),
            pl.BlockSpec((D, tn), lambda i, j: (0, j)),
        ],
        out_specs=pl.BlockSpec((tm, tn), lambda i, j: (i, j)),
        out_shape=jax.ShapeDtypeStruct((T, N), BF16),
        scratch_shapes=[pltpu.VMEM((tm, D), BF16)],
        compiler_params=_params("arbitrary", "arbitrary"),
        name="in_proj",
    )(x2, g, w_bf16)


def _attn_kernel(q_ref, k_ref, v_ref, o_ref, qf, kf, vf, acc, m_s, l_s, *, seq):
    scale = 1.0 / math.sqrt(HEAD_DIM)
    row = lax.broadcasted_iota(jnp.int32, (BLK, BLK), 0)
    col = lax.broadcasted_iota(jnp.int32, (BLK, BLK), 1)
    prev_ok = col >= row
    cur_ok = col <= row

    qf[...] = q_ref[...].astype(F32)
    kf[...] = k_ref[...].astype(F32)
    vf[...] = v_ref[...].astype(F32)

    def block(q, kk, vv, rows, has_prev, first, last):
        s = lax.dot_general(q, kk, (((1,), (1,)), ((), ())), preferred_element_type=F32) * scale
        if has_prev:
            s0 = jnp.where(prev_ok, s[:, :BLK], NEG_INF)
            s1 = jnp.where(cur_ok, s[:, BLK:], NEG_INF)
            mx = jnp.maximum(jnp.max(s0, axis=1, keepdims=True), jnp.max(s1, axis=1, keepdims=True))
        else:
            s1 = jnp.where(cur_ok, s, NEG_INF)
            mx = jnp.max(s1, axis=1, keepdims=True)
        mx = jnp.broadcast_to(mx, (BLK, BLK))
        if first:
            m_new = mx
        else:
            m_old = m_s[rows, :]
            m_new = jnp.maximum(m_old, mx)
            alpha = jnp.exp(m_old - m_new)
        p1 = jnp.exp(s1 - m_new)
        if has_prev:
            p0 = jnp.exp(s0 - m_new)
            ps = jnp.sum(p0, axis=1, keepdims=True) + jnp.sum(p1, axis=1, keepdims=True)
            p = jnp.concatenate([p0, p1], axis=1).astype(BF16)
        else:
            ps = jnp.sum(p1, axis=1, keepdims=True)
            p = p1.astype(BF16)
        ps = jnp.broadcast_to(ps, (BLK, BLK))
        pv = jnp.dot(p, vv, preferred_element_type=F32)
        if first:
            l_new = ps
            a_new = pv
        else:
            l_new = alpha * l_s[rows, :] + ps
            a_new = alpha * acc[rows, :] + pv
        if last:
            acc[rows, :] = a_new / l_new
        else:
            m_s[rows, :] = m_new
            l_s[rows, :] = l_new
            acc[rows, :] = a_new

    block(q_ref[0:BLK, :], k_ref[0:BLK, :], v_ref[0:BLK, :], pl.ds(0, BLK), False, True, False)

    def d1_body(n, c):
        q0 = pl.multiple_of(n * BLK, BLK)
        k0 = pl.multiple_of((n - 1) * BLK, BLK)
        block(q_ref[pl.ds(q0, BLK), :], k_ref[pl.ds(k0, 2 * BLK), :], v_ref[pl.ds(k0, 2 * BLK), :],
              pl.ds(q0, BLK), True, True, False)
        return c

    lax.fori_loop(1, seq // BLK, d1_body, 0)

    for d in DILATIONS[1:]:
        last = d == DILATIONS[-1]
        nblk = seq // d // BLK

        def r_body(r, c, d=d, last=last, nblk=nblk):
            rows0 = pl.ds(r, BLK, stride=d)
            block(qf[rows0, :].astype(BF16), kf[rows0, :].astype(BF16), vf[rows0, :].astype(BF16),
                  rows0, False, False, last)

            def n_body(n, c2):
                qs = pl.ds(r + n * (BLK * d), BLK, stride=d)
                ks = pl.ds(r + (n - 1) * (BLK * d), 2 * BLK, stride=d)
                block(qf[qs, :].astype(BF16), kf[ks, :].astype(BF16), vf[ks, :].astype(BF16),
                      qs, True, False, last)
                return c2

            lax.fori_loop(1, nblk, n_body, 0)
            return c

        lax.fori_loop(0, d, r_body, 0)

    o_ref[...] = acc[...].astype(BF16)


def _attention(proj3):
    B, S, _ = proj3.shape
    H = ATTN_HEADS
    blk = lambda off: pl.BlockSpec((None, S, HEAD_DIM), lambda b, h, off=off: (b, 0, off + h))
    return pl.pallas_call(
        functools.partial(_attn_kernel, seq=S),
        grid=(B, H),
        in_specs=[blk(0), blk(H), blk(2 * H)],
        out_specs=pl.BlockSpec((None, S, HEAD_DIM), lambda b, h: (b, 0, h)),
        out_shape=jax.ShapeDtypeStruct((B, S, ATTN_WIDTH), BF16),
        scratch_shapes=[pltpu.VMEM((S, HEAD_DIM), F32) for _ in range(6)],
        compiler_params=_params("arbitrary", "arbitrary"),
        name="dilated_attention",
    )(proj3, proj3, proj3)


def _ssm_prep_kernel(lr_ref, li_ref, ldt_ref, br_ref, bi_ref, ar_ref, ai_ref, bbr_ref, bbi_ref):
    lr = lr_ref[...]
    li = li_ref[...]
    dt = jnp.exp(ldt_ref[...])
    mag = jnp.exp(lr * dt)
    abar_r = mag * jnp.cos(li * dt)
    abar_i = mag * jnp.sin(li * dt)
    num_r = abar_r - 1.0
    num_i = abar_i
    den = lr * lr + li * li
    coef_r = (num_r * lr + num_i * li) / den
    coef_i = (num_i * lr - num_r * li) / den
    br = br_ref[...]
    bi = bi_ref[...]
    ar_ref[...] = abar_r
    ai_ref[...] = -abar_i
    bbr_ref[...] = coef_r * br - coef_i * bi
    bbi_ref[...] = -(coef_r * bi + coef_i * br)


def _ssm_prep(lam_re, lam_im, log_dt, b_re, b_im):
    G, P, C = SSM_GROUPS, SSM_STATE, SSM_GROUP
    gp = jax.ShapeDtypeStruct((G, 1, P), F32)
    gcp = jax.ShapeDtypeStruct((G, C, P), F32)
    return pl.pallas_call(
        _ssm_prep_kernel,
        out_shape=(gp, gp, gcp, gcp),
        name="ssm_prep",
    )(lam_re.reshape(G, 1, P), lam_im.reshape(G, 1, P), log_dt.reshape(G, 1, 1),
      jnp.swapaxes(b_re, 1, 2), jnp.swapaxes(b_im, 1, 2))


def _block_diag_slabs(m):
    G, a, b = m.shape
    gl = SSM_LANE_BLOCK // SSM_GROUP
    m4 = m.reshape(SSM_SLABS, gl, a, b)
    eye = jnp.eye(gl, dtype=m.dtype)
    out = m4[:, :, :, None, :] * eye[None, :, None, :, None]
    return out.reshape(SSM_SLABS, gl * a, gl * b)


def _chunk_permutation():
    i = np.arange(SSM_TILE)
    j = (i % SSM_CHUNK_LEN) * SSM_CHUNKS + i // SSM_CHUNK_LEN
    perm = np.zeros((SSM_TILE, SSM_TILE), np.float32)
    perm[j, i] = 1.0
    return perm


def _ssm_kernel(u_ref, perm_ref, permt_ref, a_ref, bmat_ref, cmat_ref, d_ref, wglu_ref, bglu_ref,
                gn_ref, o_ref, up_s, bu_s, y_s, car_s):
    L = SSM_CHUNK_LEN
    NS = SLAB_STATES

    @pl.when(pl.program_id(1) == 0)
    def _():
        car_s[...] = jnp.zeros_like(car_s)

    up_s[...] = jnp.dot(perm_ref[...], u_ref[...], preferred_element_type=F32).astype(BF16)
    sub = lax.broadcasted_iota(jnp.int32, (SSM_CHUNKS, NS), 0)

    for kb in range(SSM_SLABS):
        lanes = slice(kb * SSM_LANE_BLOCK, (kb + 1) * SSM_LANE_BLOCK)
        bu_s[...] = jnp.dot(up_s[:, lanes], bmat_ref[kb], preferred_element_type=F32)
        a_r = jnp.broadcast_to(a_ref[kb, 0:1, :], (SSM_CHUNKS, NS))
        a_i = jnp.broadcast_to(a_ref[kb, 1:2, :], (SSM_CHUNKS, NS))

        def step(t, s, store):
            s_r, s_i = s
            rows = pl.ds(pl.multiple_of(t * SSM_CHUNKS, SSM_CHUNKS), SSM_CHUNKS)
            b_r = bu_s[rows, 0:NS]
            b_i = bu_s[rows, NS:2 * NS]
            n_r = a_r * s_r - a_i * s_i + b_r
            n_i = a_r * s_i + a_i * s_r + b_i
            if store:
                bu_s[rows, 0:NS] = n_r
                bu_s[rows, NS:2 * NS] = n_i
            return n_r, n_i

        zero = jnp.zeros((SSM_CHUNKS, NS), F32)
        f_r, f_i = lax.fori_loop(0, L, functools.partial(step, store=False), (zero, zero), unroll=8)

        p_r, p_i = a_ref[kb, 0:1, :], a_ref[kb, 1:2, :]
        for _ in range(int(math.log2(L))):
            p_r, p_i = p_r * p_r - p_i * p_i, 2.0 * p_r * p_i

        g_r, g_i = car_s[kb:kb + 1, 0:NS], car_s[kb:kb + 1, NS:2 * NS]
        i_r, i_i = zero, zero
        for c in range(SSM_CHUNKS):
            i_r = jnp.where(sub == c, jnp.broadcast_to(g_r, (SSM_CHUNKS, NS)), i_r)
            i_i = jnp.where(sub == c, jnp.broadcast_to(g_i, (SSM_CHUNKS, NS)), i_i)
            g_r, g_i = (f_r[c:c + 1] + p_r * g_r - p_i * g_i,
                        f_i[c:c + 1] + p_r * g_i + p_i * g_r)
        car_s[kb:kb + 1, 0:NS] = g_r
        car_s[kb:kb + 1, NS:2 * NS] = g_i

        lax.fori_loop(0, L, functools.partial(step, store=True), (i_r, i_i), unroll=8)
        y_s[:, lanes] = jnp.dot(bu_s[...].astype(BF16), cmat_ref[kb], preferred_element_type=F32)

    y = y_s[...] + d_ref[...] * up_s[...].astype(F32)
    g = 0.5 * y * (1.0 + jnp.tanh(math.sqrt(2.0 / math.pi) * (y + 0.044715 * (y * y * y))))
    gate = jax.nn.sigmoid(jnp.dot(g.astype(BF16), wglu_ref[...], preferred_element_type=F32)
                          + bglu_ref[...])
    on = _rms(g * gate, gn_ref[...]).astype(BF16)
    o_ref[...] = jnp.dot(permt_ref[...], on, preferred_element_type=F32).astype(BF16)


def _ssm(proj3, a_pack, bmat, cmat, d_skip, wglu, bglu, gnorm):
    B, S, _ = proj3.shape
    ucol = 3 * ATTN_WIDTH // SSM_WIDTH
    perm = _chunk_permutation()
    const = lambda shape: pl.BlockSpec(shape, lambda b, i: (0,) * len(shape))
    return pl.pallas_call(
        _ssm_kernel,
        grid=(B, S // SSM_TILE),
        in_specs=[
            pl.BlockSpec((None, SSM_TILE, SSM_WIDTH), lambda b, i: (b, i, ucol)),
            const((SSM_TILE, SSM_TILE)),
            const((SSM_TILE, SSM_TILE)),
            const((SSM_SLABS, 2, SLAB_STATES)),
            const((SSM_SLABS, SSM_LANE_BLOCK, 2 * SLAB_STATES)),
            const((SSM_SLABS, 2 * SLAB_STATES, SSM_LANE_BLOCK)),
            const((1, SSM_WIDTH)),
            const((SSM_WIDTH, SSM_WIDTH)),
            const((1, SSM_WIDTH)),
            const((1, SSM_WIDTH)),
        ],
        out_specs=pl.BlockSpec((None, SSM_TILE, SSM_WIDTH), lambda b, i: (b, i, 0)),
        out_shape=jax.ShapeDtypeStruct((B, S, SSM_WIDTH), BF16),
        scratch_shapes=[
            pltpu.VMEM((SSM_TILE, SSM_WIDTH), BF16),
            pltpu.VMEM((SSM_TILE, 2 * SLAB_STATES), F32),
            pltpu.VMEM((SSM_TILE, SSM_WIDTH), F32),
            pltpu.VMEM((SSM_SLABS, 2 * SLAB_STATES), F32),
        ],
        compiler_params=_params("arbitrary", "arbitrary"),
        name="s5_scan",
    )(proj3, jnp.asarray(perm, BF16), jnp.asarray(perm.T, BF16), a_pack, bmat, cmat,
      d_skip, wglu, bglu, gnorm)


def _mix_kernel(x_ref, attn_ref, ssm_ref, ga_ref, w_ref, gp_ref, o_ref):
    an = _rms(attn_ref[...].astype(F32), ga_ref[...]).astype(BF16)
    mixed = jnp.concatenate([an, ssm_ref[...]], axis=1)
    y = jnp.dot(mixed, w_ref[...], preferred_element_type=F32)
    o_ref[...] = x_ref[...] + _rms(y, gp_ref[...])


def _mix(x2, attn2, ssm2, g_attn, w_out, g_post, tm=512):
    T, D = x2.shape
    row = lambda w: pl.BlockSpec((tm, w), lambda i: (i, 0))
    const = lambda shape: pl.BlockSpec(shape, lambda i: (0, 0))
    return pl.pallas_call(
        _mix_kernel,
        grid=(T // tm,),
        in_specs=[row(D), row(ATTN_WIDTH), row(SSM_WIDTH), const((1, ATTN_WIDTH)),
                  const((D, D)), const((1, D))],
        out_specs=row(D),
        out_shape=jax.ShapeDtypeStruct((T, D), F32),
        compiler_params=_params("arbitrary"),
        name="mix_out",
    )(x2, attn2, ssm2, g_attn, w_out, g_post)


def _mlp_kernel(h_ref, gpre_ref, wup_ref, wdn_ref, gpost_ref, o_ref, hn_ref, acc_ref):
    j = pl.program_id(1)

    @pl.when(j == 0)
    def _():
        hn_ref[...] = _rms(h_ref[...], gpre_ref[...]).astype(BF16)

    a = jnp.dot(hn_ref[...], wup_ref[...], preferred_element_type=F32)
    a = jnp.square(jnp.maximum(a, 0.0)).astype(BF16)
    part = jnp.dot(a, wdn_ref[...], preferred_element_type=F32)

    @pl.when(j == 0)
    def _():
        acc_ref[...] = part

    @pl.when(j > 0)
    def _():
        acc_ref[...] += part

    @pl.when(j == pl.num_programs(1) - 1)
    def _():
        o_ref[...] = h_ref[...] + _rms(acc_ref[...], gpost_ref[...])


def _mlp(h, g_pre, w_up, w_down, g_post, tm=512, tf=1024):
    T, D = h.shape
    F = w_up.shape[1]
    return pl.pallas_call(
        _mlp_kernel,
        grid=(T // tm, F // tf),
        in_specs=[
            pl.BlockSpec((tm, D), lambda i, j: (i, 0)),
            pl.BlockSpec((1, D), lambda i, j: (0, 0)),
            pl.BlockSpec((D, tf), lambda i, j: (0, j)),
            pl.BlockSpec((tf, D), lambda i, j: (j, 0)),
            pl.BlockSpec((1, D), lambda i, j: (0, 0)),
        ],
        out_specs=pl.BlockSpec((tm, D), lambda i, j: (i, 0)),
        out_shape=jax.ShapeDtypeStruct((T, D), F32),
        scratch_shapes=[pltpu.VMEM((tm, D), BF16), pltpu.VMEM((tm, D), F32)],
        compiler_params=_params("arbitrary", "arbitrary"),
        name="relu2_mlp",
    )(h, g_pre, w_up, w_down, g_post)


def _ple_kernel(h_ref, p_ref, gpre_ref, wg_ref, wp_ref, gpost_ref, o_ref):
    h = h_ref[...]
    hn = _rms(h, gpre_ref[...]).astype(BF16)
    gate = jax.nn.sigmoid(jnp.dot(hn, wg_ref[...], preferred_element_type=F32))
    e = jnp.dot(p_ref[...].astype(BF16), wp_ref[...], preferred_element_type=F32)
    o_ref[...] = h + _rms(gate * e, gpost_ref[...])


def _ple(h, p2, g_pre, w_gate, w_proj, g_post, tm=512):
    T, D = h.shape
    row = lambda w: pl.BlockSpec((tm, w), lambda i: (i, 0))
    const = lambda shape: pl.BlockSpec(shape, lambda i: (0, 0))
    return pl.pallas_call(
        _ple_kernel,
        grid=(T // tm,),
        in_specs=[row(D), row(PLE_DIM), const((1, D)), const((D, D)), const((PLE_DIM, D)),
                  const((1, D))],
        out_specs=row(D),
        out_shape=jax.ShapeDtypeStruct((T, D), F32),
        compiler_params=_params("arbitrary"),
        name="ple_gate",
    )(h, p2, g_pre, w_gate, w_proj, g_post)


def kernel(x, p, mix_norm_pre, w_in, lam_re, lam_im, log_dt, ssm_b_re, ssm_b_im, ssm_c_re, ssm_c_im, ssm_d, w_glu, b_glu, attn_out_norm, ssm_out_norm, w_out, mix_norm_post, mlp_norm_pre, w_up, w_down, mlp_norm_post, ple_norm_pre, w_ple_gate, w_ple_proj, ple_norm_post):
    B, S, D = x.shape
    T = B * S
    h = x.reshape(T, D)
    for i in range(p.shape[0]):
        vec = lambda a: a[i].reshape(1, -1)
        proj = _in_proj(h, vec(mix_norm_pre), w_in[i].astype(BF16))
        proj3 = proj.reshape(B, S, -1)
        attn = _attention(proj3)

        a_r, a_i, bb_r, bb_i = _ssm_prep(lam_re[i], lam_im[i], log_dt[i], ssm_b_re[i], ssm_b_im[i])
        a_pack = jnp.stack([a_r.reshape(SSM_SLABS, SLAB_STATES), a_i.reshape(SSM_SLABS, SLAB_STATES)],
                           axis=1)
        bmat = jnp.concatenate([_block_diag_slabs(bb_r), _block_diag_slabs(bb_i)], axis=2).astype(BF16)
        cmat = jnp.concatenate([_block_diag_slabs(jnp.swapaxes(ssm_c_re[i], 1, 2)),
                                _block_diag_slabs(jnp.swapaxes(ssm_c_im[i], 1, 2))], axis=1).astype(BF16)
        ssm = _ssm(proj3, a_pack, bmat, cmat, vec(ssm_d), w_glu[i].astype(BF16), vec(b_glu),
                   vec(ssm_out_norm))

        h = _mix(h, attn.reshape(T, -1), ssm.reshape(T, -1), vec(attn_out_norm),
                 w_out[i].astype(BF16), vec(mix_norm_post))
        h = _mlp(h, vec(mlp_norm_pre), w_up[i].astype(BF16), w_down[i].astype(BF16),
                 vec(mlp_norm_post))
        h = _ple(h, p[i].reshape(T, -1), vec(ple_norm_pre), w_ple_gate[i].astype(BF16),
                 w_ple_proj[i].astype(BF16), vec(ple_norm_post))
    return h.reshape(B, S, D)
```

```python
import functools
import math

import numpy as np
import jax
import jax.numpy as jnp
from jax import lax
from jax.experimental import pallas as pl
from jax.experimental.pallas import tpu as pltpu

F32 = jnp.float32
BF16 = jnp.bfloat16

D_MODEL = 2048
HEAD_DIM = 128
ATTN_WIDTH = 1024
ATTN_HEADS = ATTN_WIDTH // HEAD_DIM
SSM_WIDTH = 1024
SSM_GROUP = 16
SSM_GROUPS = SSM_WIDTH // SSM_GROUP
SSM_STATE = 64
DILATIONS = (1, 4, 16)
BLK = 128
ATTN_GROUP = 8
D_FF = 4 * D_MODEL
PLE_DIM = 256
RMS_EPS = 1e-6
NEG_INF = -1e30

V7X_VMEM_BYTES = 64 * 1024 * 1024
VMEM_LIMIT = 52 * 1024 * 1024

SSM_TILE = 512
SSM_CHUNKS = 8
SSM_CHUNK_LEN = SSM_TILE // SSM_CHUNKS
SSM_LANE_BLOCK = 128
SSM_SLABS = SSM_WIDTH // SSM_LANE_BLOCK
SLAB_STATES = (SSM_LANE_BLOCK // SSM_GROUP) * SSM_STATE


def _rms(x, g):
    ms = jnp.mean(x * x, axis=-1, keepdims=True)
    return x * lax.rsqrt(ms + RMS_EPS) * g


def _params(*sem):
    return pltpu.CompilerParams(dimension_semantics=sem, vmem_limit_bytes=VMEM_LIMIT)


def _in_proj_kernel(x_ref, g_ref, w_ref, o_ref, xn_ref):
    @pl.when(pl.program_id(1) == 0)
    def _():
        xn_ref[...] = _rms(x_ref[...], g_ref[...]).astype(BF16)

    o_ref[...] = jnp.dot(xn_ref[...], w_ref[...], preferred_element_type=F32).astype(BF16)


def _in_proj(x2, g, w_bf16, tm=1024, tn=1024):
    T, D = x2.shape
    N = w_bf16.shape[1]
    return pl.pallas_call(
        _in_proj_kernel,
        grid=(T // tm, N // tn),
        in_specs=[
            pl.BlockSpec((tm, D), lambda i, j: (i, 0)),
            pl.BlockSpec((1, D), lambda i, j: (0, 0)),
            pl.BlockSpec((D, tn), lambda i, j: (0, j)),
        ],
        out_specs=pl.BlockSpec((tm, tn), lambda i, j: (i, j)),
        out_shape=jax.ShapeDtypeStruct((T, N), BF16),
        scratch_shapes=[pltpu.VMEM((tm, D), BF16)],
        compiler_params=_params("arbitrary", "arbitrary"),
        name="in_proj",
    )(x2, g, w_bf16)


def _attn_kernel(q_ref, k_ref, v_ref, o_ref, qf, kf, vf, q4, k4, v4, qd, kd, vd,
                 a1, m1, d1, a4, m4, d4, *, seq):
    scale = 1.0 / math.sqrt(HEAD_DIM)
    row = lax.broadcasted_iota(jnp.int32, (BLK, BLK), 0)
    col = lax.broadcasted_iota(jnp.int32, (BLK, BLK), 1)
    cur_ok = col <= row
    both_ok = jnp.concatenate([col >= row, cur_ok], axis=1)
    ones = jnp.ones((BLK, BLK), BF16)

    def banded_group(base, qs, ks, vs, outs, has_prev, dst):
        acc_o, max_o, den_o = outs
        nb = len(has_prev)
        row0 = [base + b * BLK for b in range(nb)]
        s_all, m_all = [], []
        for b in range(nb):
            q = qs[pl.ds(row0[b], BLK), :]
            if has_prev[b]:
                kk = ks[pl.ds(row0[b] - BLK, 2 * BLK), :]
                mask = both_ok
            else:
                kk = ks[pl.ds(row0[b], BLK), :]
                mask = cur_ok
            s = lax.dot_general(q, kk, (((1,), (1,)), ((), ())), preferred_element_type=F32)
            s = jnp.where(mask, s * scale, NEG_INF)
            if has_prev[b]:
                m = jnp.max(jnp.maximum(s[:, :BLK], s[:, BLK:]), axis=1, keepdims=True)
            else:
                m = jnp.max(s, axis=1, keepdims=True)
            s_all.append(s)
            m_all.append(m)
        p_all = [jnp.exp(s - m).astype(BF16) for s, m in zip(s_all, m_all)]
        for b in range(nb):
            if has_prev[b]:
                vv = vs[pl.ds(row0[b] - BLK, 2 * BLK), :]
                aug = jnp.concatenate([vv, jnp.concatenate([ones, ones], axis=0)], axis=1)
            else:
                vv = vs[pl.ds(row0[b], BLK), :]
                aug = jnp.concatenate([vv, ones], axis=1)
            pv = jnp.dot(p_all[b], aug, preferred_element_type=F32)
            rows = dst(b)
            acc_o[rows, :] = pv[:, :BLK]
            den_o[rows, :] = pv[:, BLK:]
            max_o[rows, :] = jnp.broadcast_to(m_all[b], (BLK, BLK))

    def banded(qs, ks, vs, outs, blocks_per_segment, dst_of=None):
        ngroups = seq // BLK // ATTN_GROUP
        span = ATTN_GROUP * BLK
        if dst_of is None:
            dst_of = lambda g, base: (lambda b: pl.ds(base + b * BLK, BLK))

        def run(g, has_prev):
            base = g * span if isinstance(g, int) else pl.multiple_of(g * span, span)
            banded_group(base, qs, ks, vs, outs, has_prev, dst_of(g, base))

        def loop(lo, has_prev):
            def body(g, c):
                run(g, has_prev)
                return c
            lax.fori_loop(lo, ngroups, body, 0)

        if blocks_per_segment == ngroups * ATTN_GROUP:
            run(0, [False] + [True] * (ATTN_GROUP - 1))
            loop(1, [True] * ATTN_GROUP)
        else:
            assert ATTN_GROUP % blocks_per_segment == 0
            loop(0, [b % blocks_per_segment != 0 for b in range(ATTN_GROUP)])

    seq4 = seq // 4

    banded(q_ref, k_ref, v_ref, (a1, m1, d1), seq // BLK)

    qf[...] = q_ref[...].astype(F32)
    kf[...] = k_ref[...].astype(F32)
    vf[...] = v_ref[...].astype(F32)

    def gather4(r, c):
        src = pl.ds(r, seq4, stride=4)
        dst = pl.ds(pl.multiple_of(r * seq4, seq4), seq4)
        for f32_src, f32_dst, bf16_dst in ((qf, q4, qd), (kf, k4, kd), (vf, v4, vd)):
            x = f32_src[src, :]
            f32_dst[dst, :] = x
            bf16_dst[dst, :] = x.astype(BF16)
        return c

    lax.fori_loop(0, 4, gather4, 0)
    banded(qd, kd, vd, (a4, m4, d4), seq4 // BLK)

    seq16 = seq // 16

    def gather16(r, c):
        src = pl.ds((r % 4) * seq4 + r // 4, seq16, stride=4)
        dst = pl.ds(pl.multiple_of(r * seq16, seq16), seq16)
        qd[dst, :] = q4[src, :].astype(BF16)
        kd[dst, :] = k4[src, :].astype(BF16)
        vd[dst, :] = v4[src, :].astype(BF16)
        return c

    lax.fori_loop(0, 16, gather16, 0)

    blocks16 = seq16 // BLK
    a16, m16, d16 = qf, kf, vf

    def dst16(g, base):
        return lambda b: pl.ds((b // blocks16) * seq4 + g + 4 * BLK * (b % blocks16), BLK, stride=4)

    banded(qd, kd, vd, (a16, m16, d16), blocks16, dst16)

    out = q4
    chunk = 256

    def mix(i, c):
        r = i // (seq4 // chunk)
        j = i % (seq4 // chunk)
        nat = pl.ds(r + 4 * chunk * j, chunk, stride=4)
        loc = pl.ds(pl.multiple_of(i * chunk, chunk), chunk)
        x1, x4, x16 = m1[nat, :], m4[loc, :], m16[loc, :]
        top = jnp.maximum(jnp.maximum(x1, x4), x16)
        e1, e4, e16 = jnp.exp(x1 - top), jnp.exp(x4 - top), jnp.exp(x16 - top)
        num = e1 * a1[nat, :] + e4 * a4[loc, :] + e16 * a16[loc, :]
        den = e1 * d1[nat, :] + e4 * d4[loc, :] + e16 * d16[loc, :]
        out[nat, :] = num / den
        return c

    lax.fori_loop(0, seq // chunk, mix, 0)
    o_ref[...] = out[...].astype(BF16)


def _attention(proj3):
    B, S, _ = proj3.shape
    H = ATTN_HEADS
    blk = lambda off: pl.BlockSpec((None, S, HEAD_DIM), lambda b, h, off=off: (b, 0, off + h))
    f32buf = pltpu.VMEM((S, HEAD_DIM), F32)
    bf16buf = pltpu.VMEM((S, HEAD_DIM), BF16)
    return pl.pallas_call(
        functools.partial(_attn_kernel, seq=S),
        grid=(B, H),
        in_specs=[blk(0), blk(H), blk(2 * H)],
        out_specs=pl.BlockSpec((None, S, HEAD_DIM), lambda b, h: (b, 0, h)),
        out_shape=jax.ShapeDtypeStruct((B, S, ATTN_WIDTH), BF16),
        scratch_shapes=[f32buf] * 6 + [bf16buf] * 3 + [f32buf] * 6,
        compiler_params=_params("arbitrary", "arbitrary"),
        name="dilated_attention",
    )(proj3, proj3, proj3)


def _ssm_prep_kernel(lr_ref, li_ref, ldt_ref, br_ref, bi_ref, ar_ref, ai_ref, bbr_ref, bbi_ref):
    lr = lr_ref[...]
    li = li_ref[...]
    dt = jnp.exp(ldt_ref[...])
    mag = jnp.exp(lr * dt)
    abar_r = mag * jnp.cos(li * dt)
    abar_i = mag * jnp.sin(li * dt)
    num_r = abar_r - 1.0
    num_i = abar_i
    den = lr * lr + li * li
    coef_r = (num_r * lr + num_i * li) / den
    coef_i = (num_i * lr - num_r * li) / den
    br = br_ref[...]
    bi = bi_ref[...]
    ar_ref[...] = abar_r
    ai_ref[...] = -abar_i
    bbr_ref[...] = coef_r * br - coef_i * bi
    bbi_ref[...] = -(coef_r * bi + coef_i * br)


def _ssm_prep(lam_re, lam_im, log_dt, b_re, b_im):
    G, P, C = SSM_GROUPS, SSM_STATE, SSM_GROUP
    gp = jax.ShapeDtypeStruct((G, 1, P), F32)
    gcp = jax.ShapeDtypeStruct((G, C, P), F32)
    return pl.pallas_call(
        _ssm_prep_kernel,
        out_shape=(gp, gp, gcp, gcp),
        name="ssm_prep",
    )(lam_re.reshape(G, 1, P), lam_im.reshape(G, 1, P), log_dt.reshape(G, 1, 1),
      jnp.swapaxes(b_re, 1, 2), jnp.swapaxes(b_im, 1, 2))


def _block_diag_slabs(m):
    G, a, b = m.shape
    gl = SSM_LANE_BLOCK // SSM_GROUP
    m4 = m.reshape(SSM_SLABS, gl, a, b)
    eye = jnp.eye(gl, dtype=m.dtype)
    out = m4[:, :, :, None, :] * eye[None, :, None, :, None]
    return out.reshape(SSM_SLABS, gl * a, gl * b)


def _chunk_permutation():
    i = np.arange(SSM_TILE)
    j = (i % SSM_CHUNK_LEN) * SSM_CHUNKS + i // SSM_CHUNK_LEN
    perm = np.zeros((SSM_TILE, SSM_TILE), np.float32)
    perm[j, i] = 1.0
    return perm


def _ssm_kernel(u_ref, perm_ref, permt_ref, a_ref, bmat_ref, cmat_ref, d_ref, wglu_ref, bglu_ref,
                gn_ref, o_ref, up_s, bu_s, y_s, car_s):
    L = SSM_CHUNK_LEN
    NS = SLAB_STATES

    @pl.when(pl.program_id(1) == 0)
    def _():
        car_s[...] = jnp.zeros_like(car_s)

    up_s[...] = jnp.dot(perm_ref[...], u_ref[...], preferred_element_type=F32).astype(BF16)
    sub = lax.broadcasted_iota(jnp.int32, (SSM_CHUNKS, NS), 0)

    for kb in range(SSM_SLABS):
        lanes = slice(kb * SSM_LANE_BLOCK, (kb + 1) * SSM_LANE_BLOCK)
        bu_s[...] = jnp.dot(up_s[:, lanes], bmat_ref[kb], preferred_element_type=F32)
        a_r = jnp.broadcast_to(a_ref[kb, 0:1, :], (SSM_CHUNKS, NS))
        a_i = jnp.broadcast_to(a_ref[kb, 1:2, :], (SSM_CHUNKS, NS))

        def step(t, s, store):
            s_r, s_i = s
            rows = pl.ds(pl.multiple_of(t * SSM_CHUNKS, SSM_CHUNKS), SSM_CHUNKS)
            b_r = bu_s[rows, 0:NS]
            b_i = bu_s[rows, NS:2 * NS]
            n_r = a_r * s_r - a_i * s_i + b_r
            n_i = a_r * s_i + a_i * s_r + b_i
            if store:
                bu_s[rows, 0:NS] = n_r
                bu_s[rows, NS:2 * NS] = n_i
            return n_r, n_i

        zero = jnp.zeros((SSM_CHUNKS, NS), F32)
        f_r, f_i = lax.fori_loop(0, L, functools.partial(step, store=False), (zero, zero), unroll=8)

        p_r, p_i = a_ref[kb, 0:1, :], a_ref[kb, 1:2, :]
        for _ in range(int(math.log2(L))):
            p_r, p_i = p_r * p_r - p_i * p_i, 2.0 * p_r * p_i

        g_r, g_i = car_s[kb:kb + 1, 0:NS], car_s[kb:kb + 1, NS:2 * NS]
        i_r, i_i = zero, zero
        for c in range(SSM_CHUNKS):
            i_r = jnp.where(sub == c, jnp.broadcast_to(g_r, (SSM_CHUNKS, NS)), i_r)
            i_i = jnp.where(sub == c, jnp.broadcast_to(g_i, (SSM_CHUNKS, NS)), i_i)
            g_r, g_i = (f_r[c:c + 1] + p_r * g_r - p_i * g_i,
                        f_i[c:c + 1] + p_r * g_i + p_i * g_r)
        car_s[kb:kb + 1, 0:NS] = g_r
        car_s[kb:kb + 1, NS:2 * NS] = g_i

        lax.fori_loop(0, L, functools.partial(step, store=True), (i_r, i_i), unroll=8)
        y_s[:, lanes] = jnp.dot(bu_s[...].astype(BF16), cmat_ref[kb], preferred_element_type=F32)

    y = y_s[...] + d_ref[...] * up_s[...].astype(F32)
    g = 0.5 * y * (1.0 + jnp.tanh(math.sqrt(2.0 / math.pi) * (y + 0.044715 * (y * y * y))))
    gate = jax.nn.sigmoid(jnp.dot(g.astype(BF16), wglu_ref[...], preferred_element_type=F32)
                          + bglu_ref[...])
    on = _rms(g * gate, gn_ref[...]).astype(BF16)
    o_ref[...] = jnp.dot(permt_ref[...], on, preferred_element_type=F32).astype(BF16)


def _ssm(proj3, a_pack, bmat, cmat, d_skip, wglu, bglu, gnorm):
    B, S, _ = proj3.shape
    ucol = 3 * ATTN_WIDTH // SSM_WIDTH
    perm = _chunk_permutation()
    const = lambda shape: pl.BlockSpec(shape, lambda b, i: (0,) * len(shape))
    return pl.pallas_call(
        _ssm_kernel,
        grid=(B, S // SSM_TILE),
        in_specs=[
            pl.BlockSpec((None, SSM_TILE, SSM_WIDTH), lambda b, i: (b, i, ucol)),
            const((SSM_TILE, SSM_TILE)),
            const((SSM_TILE, SSM_TILE)),
            const((SSM_SLABS, 2, SLAB_STATES)),
            const((SSM_SLABS, SSM_LANE_BLOCK, 2 * SLAB_STATES)),
            const((SSM_SLABS, 2 * SLAB_STATES, SSM_LANE_BLOCK)),
            const((1, SSM_WIDTH)),
            const((SSM_WIDTH, SSM_WIDTH)),
            const((1, SSM_WIDTH)),
            const((1, SSM_WIDTH)),
        ],
        out_specs=pl.BlockSpec((None, SSM_TILE, SSM_WIDTH), lambda b, i: (b, i, 0)),
        out_shape=jax.ShapeDtypeStruct((B, S, SSM_WIDTH), BF16),
        scratch_shapes=[
            pltpu.VMEM((SSM_TILE, SSM_WIDTH), BF16),
            pltpu.VMEM((SSM_TILE, 2 * SLAB_STATES), F32),
            pltpu.VMEM((SSM_TILE, SSM_WIDTH), F32),
            pltpu.VMEM((SSM_SLABS, 2 * SLAB_STATES), F32),
        ],
        compiler_params=_params("arbitrary", "arbitrary"),
        name="s5_scan",
    )(proj3, jnp.asarray(perm, BF16), jnp.asarray(perm.T, BF16), a_pack, bmat, cmat,
      d_skip, wglu, bglu, gnorm)


def _mix_kernel(x_ref, attn_ref, ssm_ref, ga_ref, w_ref, gp_ref, o_ref):
    an = _rms(attn_ref[...].astype(F32), ga_ref[...]).astype(BF16)
    mixed = jnp.concatenate([an, ssm_ref[...]], axis=1)
    y = jnp.dot(mixed, w_ref[...], preferred_element_type=F32)
    o_ref[...] = x_ref[...] + _rms(y, gp_ref[...])


def _mix(x2, attn2, ssm2, g_attn, w_out, g_post, tm=512):
    T, D = x2.shape
    row = lambda w: pl.BlockSpec((tm, w), lambda i: (i, 0))
    const = lambda shape: pl.BlockSpec(shape, lambda i: (0, 0))
    return pl.pallas_call(
        _mix_kernel,
        grid=(T // tm,),
        in_specs=[row(D), row(ATTN_WIDTH), row(SSM_WIDTH), const((1, ATTN_WIDTH)),
                  const((D, D)), const((1, D))],
        out_specs=row(D),
        out_shape=jax.ShapeDtypeStruct((T, D), F32),
        compiler_params=_params("arbitrary"),
        name="mix_out",
    )(x2, attn2, ssm2, g_attn, w_out, g_post)


def _mlp_kernel(h_ref, gpre_ref, wup_ref, wdn_ref, gpost_ref, o_ref, hn_ref, acc_ref):
    j = pl.program_id(1)

    @pl.when(j == 0)
    def _():
        hn_ref[...] = _rms(h_ref[...], gpre_ref[...]).astype(BF16)
        acc_ref[...] = jnp.zeros_like(acc_ref)

    a = jnp.dot(hn_ref[...], wup_ref[...], preferred_element_type=F32)
    a = jnp.square(jnp.maximum(a, 0.0)).astype(BF16)
    acc_ref[...] += jnp.dot(a, wdn_ref[...], preferred_element_type=F32)

    @pl.when(j == pl.num_programs(1) - 1)
    def _():
        o_ref[...] = h_ref[...] + _rms(acc_ref[...], gpost_ref[...])


def _mlp(h, g_pre, w_up, w_down, g_post, tm=512, tf=1024):
    T, D = h.shape
    F = w_up.shape[1]
    return pl.pallas_call(
        _mlp_kernel,
        grid=(T // tm, F // tf),
        in_specs=[
            pl.BlockSpec((tm, D), lambda i, j: (i, 0)),
            pl.BlockSpec((1, D), lambda i, j: (0, 0)),
            pl.BlockSpec((D, tf), lambda i, j: (0, j)),
            pl.BlockSpec((tf, D), lambda i, j: (j, 0)),
            pl.BlockSpec((1, D), lambda i, j: (0, 0)),
        ],
        out_specs=pl.BlockSpec((tm, D), lambda i, j: (i, 0)),
        out_shape=jax.ShapeDtypeStruct((T, D), F32),
        scratch_shapes=[pltpu.VMEM((tm, D), BF16), pltpu.VMEM((tm, D), F32)],
        compiler_params=_params("arbitrary", "arbitrary"),
        name="relu2_mlp",
    )(h, g_pre, w_up, w_down, g_post)


def _ple_kernel(h_ref, p_ref, gpre_ref, wg_ref, wp_ref, gpost_ref, o_ref):
    h = h_ref[...]
    hn = _rms(h, gpre_ref[...]).astype(BF16)
    gate = jax.nn.sigmoid(jnp.dot(hn, wg_ref[...], preferred_element_type=F32))
    e = jnp.dot(p_ref[...].astype(BF16), wp_ref[...], preferred_element_type=F32)
    o_ref[...] = h + _rms(gate * e, gpost_ref[...])


def _ple(h, p2, g_pre, w_gate, w_proj, g_post, tm=512):
    T, D = h.shape
    row = lambda w: pl.BlockSpec((tm, w), lambda i: (i, 0))
    const = lambda shape: pl.BlockSpec(shape, lambda i: (0, 0))
    return pl.pallas_call(
        _ple_kernel,
        grid=(T // tm,),
        in_specs=[row(D), row(PLE_DIM), const((1, D)), const((D, D)), const((PLE_DIM, D)),
                  const((1, D))],
        out_specs=row(D),
        out_shape=jax.ShapeDtypeStruct((T, D), F32),
        compiler_params=_params("arbitrary"),
        name="ple_gate",
    )(h, p2, g_pre, w_gate, w_proj, g_post)


def kernel(x, p, mix_norm_pre, w_in, lam_re, lam_im, log_dt, ssm_b_re, ssm_b_im, ssm_c_re, ssm_c_im, ssm_d, w_glu, b_glu, attn_out_norm, ssm_out_norm, w_out, mix_norm_post, mlp_norm_pre, w_up, w_down, mlp_norm_post, ple_norm_pre, w_ple_gate, w_ple_proj, ple_norm_post):
    B, S, D = x.shape
    T = B * S
    h = x.reshape(T, D)
    for i in range(p.shape[0]):
        vec = lambda a: a[i].reshape(1, -1)
        proj = _in_proj(h, vec(mix_norm_pre), w_in[i].astype(BF16))
        proj3 = proj.reshape(B, S, -1)
        attn = _attention(proj3)

        a_r, a_i, bb_r, bb_i = _ssm_prep(lam_re[i], lam_im[i], log_dt[i], ssm_b_re[i], ssm_b_im[i])
        a_pack = jnp.stack([a_r.reshape(SSM_SLABS, SLAB_STATES), a_i.reshape(SSM_SLABS, SLAB_STATES)],
                           axis=1)
        bmat = jnp.concatenate([_block_diag_slabs(bb_r), _block_diag_slabs(bb_i)], axis=2).astype(BF16)
        cmat = jnp.concatenate([_block_diag_slabs(jnp.swapaxes(ssm_c_re[i], 1, 2)),
                                _block_diag_slabs(jnp.swapaxes(ssm_c_im[i], 1, 2))], axis=1).astype(BF16)
        ssm = _ssm(proj3, a_pack, bmat, cmat, vec(ssm_d), w_glu[i].astype(BF16), vec(b_glu),
                   vec(ssm_out_norm))

        h = _mix(h, attn.reshape(T, -1), ssm.reshape(T, -1), vec(attn_out_norm),
                 w_out[i].astype(BF16), vec(mix_norm_post))
        h = _mlp(h, vec(mlp_norm_pre), w_up[i].astype(BF16), w_down[i].astype(BF16),
                 vec(mlp_norm_post))
        h = _ple(h, p[i].reshape(T, -1), vec(ple_norm_pre), w_ple_gate[i].astype(BF16),
                 w_ple_proj[i].astype(BF16), vec(ple_norm_post))
    return h.reshape(B, S, D)
```

```python
import functools
import math

import numpy as np
import jax
import jax.numpy as jnp
from jax import lax
from jax.experimental import pallas as pl
from jax.experimental.pallas import tpu as pltpu

F32 = jnp.float32
BF16 = jnp.bfloat16

D_MODEL = 2048
HEAD_DIM = 128
ATTN_WIDTH = 1024
ATTN_HEADS = ATTN_WIDTH // HEAD_DIM
SSM_WIDTH = 1024
SSM_GROUP = 16
SSM_GROUPS = SSM_WIDTH // SSM_GROUP
SSM_STATE = 64
DILATIONS = (1, 4, 16)
BLK = 128
ATTN_GROUP = 8
D_FF = 4 * D_MODEL
PLE_DIM = 256
RMS_EPS = 1e-6
NEG_INF = -1e30

V7X_VMEM_BYTES = 64 * 1024 * 1024
VMEM_LIMIT = 52 * 1024 * 1024

SSM_TILE = 512
SSM_CHUNKS = 8
SSM_CHUNK_LEN = SSM_TILE // SSM_CHUNKS
SSM_LANE_BLOCK = 128
SSM_SLABS = SSM_WIDTH // SSM_LANE_BLOCK
SLAB_STATES = (SSM_LANE_BLOCK // SSM_GROUP) * SSM_STATE


def _rms(x, g):
    ms = jnp.mean(x * x, axis=-1, keepdims=True)
    return x * lax.rsqrt(ms + RMS_EPS) * g


def _params(*sem, flags=None):
    return pltpu.CompilerParams(dimension_semantics=sem, vmem_limit_bytes=VMEM_LIMIT, flags=flags)


def _in_proj_kernel(x_ref, g_ref, w_ref, *refs, n_cast):
    cast_in, o_ref, cast_out, xn_ref = (refs[:n_cast], refs[n_cast], refs[n_cast + 1:2 * n_cast + 1],
                                        refs[2 * n_cast + 1])
    j = pl.program_id(1)

    @pl.when(j == 0)
    def _():
        xn_ref[...] = _rms(x_ref[...], g_ref[...]).astype(BF16)

    for src, dst in zip(cast_in, cast_out):
        dst[...] = src[...].astype(BF16)
    col_scale = jnp.where(j == 0, 1.0 / math.sqrt(HEAD_DIM), 1.0)
    acc = jnp.dot(xn_ref[...], w_ref[...], preferred_element_type=F32)
    o_ref[...] = (acc * col_scale).astype(BF16)


def _in_proj(x2, g, w_bf16, to_cast, tm=1024, tn=ATTN_WIDTH):
    T, D = x2.shape
    N = w_bf16.shape[1]
    grid = (T // tm, N // tn)
    steps = grid[0] * grid[1]
    flat = [a.reshape(-1, D) for a in to_cast]
    step_of = lambda i, j: (i * grid[1] + j, 0)
    cast_specs = [pl.BlockSpec((a.shape[0] // steps, D), step_of) for a in flat]
    outs = pl.pallas_call(
        functools.partial(_in_proj_kernel, n_cast=len(flat)),
        grid=grid,
        in_specs=[
            pl.BlockSpec((tm, D), lambda i, j: (i, 0)),
            pl.BlockSpec((1, D), lambda i, j: (0, 0)),
            pl.BlockSpec((D, tn), lambda i, j: (0, j)),
        ] + cast_specs,
        out_specs=[pl.BlockSpec((tm, tn), lambda i, j: (i, j))] + cast_specs,
        out_shape=[jax.ShapeDtypeStruct((T, N), BF16)]
                  + [jax.ShapeDtypeStruct(a.shape, BF16) for a in flat],
        scratch_shapes=[pltpu.VMEM((tm, D), BF16)],
        compiler_params=_params("arbitrary", "arbitrary"),
        name="in_proj",
    )(x2, g, w_bf16, *flat)
    return outs[0], [o.reshape(a.shape) for o, a in zip(outs[1:], to_cast)]


def _attn_kernel(q_ref, k_ref, v_ref, o_ref, qf, kf, vf, q4, k4, v4, qd, kd, vd,
                 a1, m1, d1, a4, m4, d4, *, seq):
    row = lax.broadcasted_iota(jnp.int32, (BLK, BLK), 0)
    col = lax.broadcasted_iota(jnp.int32, (BLK, BLK), 1)
    cur_ok = col <= row
    both_ok = jnp.concatenate([col >= row, cur_ok], axis=1)
    ones = jnp.ones((BLK, BLK), BF16)

    def banded_group(base, qs, ks, vs, outs, has_prev, dst):
        acc_o, max_o, den_o = outs
        nb = len(has_prev)
        row0 = [base + b * BLK for b in range(nb)]
        s_all, m_all = [], []
        for b in range(nb):
            q = qs[pl.ds(row0[b], BLK), :]
            if has_prev[b]:
                kk = ks[pl.ds(row0[b] - BLK, 2 * BLK), :]
                mask = both_ok
            else:
                kk = ks[pl.ds(row0[b], BLK), :]
                mask = cur_ok
            s = lax.dot_general(q, kk, (((1,), (1,)), ((), ())), preferred_element_type=F32)
            s = jnp.where(mask, s, NEG_INF)
            if has_prev[b]:
                m = jnp.max(jnp.maximum(s[:, :BLK], s[:, BLK:]), axis=1, keepdims=True)
            else:
                m = jnp.max(s, axis=1, keepdims=True)
            s_all.append(s)
            m_all.append(m)
        p_all = [jnp.exp(s - m).astype(BF16) for s, m in zip(s_all, m_all)]
        for b in range(nb):
            if has_prev[b]:
                vv = vs[pl.ds(row0[b] - BLK, 2 * BLK), :]
                aug = jnp.concatenate([vv, jnp.concatenate([ones, ones], axis=0)], axis=1)
            else:
                vv = vs[pl.ds(row0[b], BLK), :]
                aug = jnp.concatenate([vv, ones], axis=1)
            pv = jnp.dot(p_all[b], aug, preferred_element_type=F32)
            rows = dst(b)
            acc_o[rows, :] = pv[:, :BLK]
            den_o[rows, :] = pv[:, BLK:]
            max_o[rows, :] = jnp.broadcast_to(m_all[b], (BLK, BLK))

    def banded(qs, ks, vs, outs, blocks_per_segment, dst_of=None):
        ngroups = seq // BLK // ATTN_GROUP
        span = ATTN_GROUP * BLK
        if dst_of is None:
            dst_of = lambda g, base: (lambda b: pl.ds(base + b * BLK, BLK))

        def run(g, has_prev):
            base = g * span if isinstance(g, int) else pl.multiple_of(g * span, span)
            banded_group(base, qs, ks, vs, outs, has_prev, dst_of(g, base))

        def loop(lo, has_prev):
            def body(g, c):
                run(g, has_prev)
                return c
            lax.fori_loop(lo, ngroups, body, 0)

        if blocks_per_segment == ngroups * ATTN_GROUP:
            run(0, [False] + [True] * (ATTN_GROUP - 1))
            loop(1, [True] * ATTN_GROUP)
        else:
            assert ATTN_GROUP % blocks_per_segment == 0
            loop(0, [b % blocks_per_segment != 0 for b in range(ATTN_GROUP)])

    seq4 = seq // 4

    banded(q_ref, k_ref, v_ref, (a1, m1, d1), seq // BLK)

    qf[...] = q_ref[...].astype(F32)
    kf[...] = k_ref[...].astype(F32)
    vf[...] = v_ref[...].astype(F32)

    def gather4(r, c):
        src = pl.ds(r, seq4, stride=4)
        dst = pl.ds(pl.multiple_of(r * seq4, seq4), seq4)
        for f32_src, f32_dst, bf16_dst in ((qf, q4, qd), (kf, k4, kd), (vf, v4, vd)):
            x = f32_src[src, :]
            f32_dst[dst, :] = x
            bf16_dst[dst, :] = x.astype(BF16)
        return c

    lax.fori_loop(0, 4, gather4, 0)
    banded(qd, kd, vd, (a4, m4, d4), seq4 // BLK)

    seq16 = seq // 16

    def gather16(r, c):
        src = pl.ds((r % 4) * seq4 + r // 4, seq16, stride=4)
        dst = pl.ds(pl.multiple_of(r * seq16, seq16), seq16)
        qd[dst, :] = q4[src, :].astype(BF16)
        kd[dst, :] = k4[src, :].astype(BF16)
        vd[dst, :] = v4[src, :].astype(BF16)
        return c

    lax.fori_loop(0, 16, gather16, 0)

    blocks16 = seq16 // BLK
    a16, m16, d16 = qf, kf, vf

    def dst16(g, base):
        return lambda b: pl.ds((b // blocks16) * seq4 + g + 4 * BLK * (b % blocks16), BLK, stride=4)

    banded(qd, kd, vd, (a16, m16, d16), blocks16, dst16)

    out = q4
    chunk = 256

    def mix(i, c):
        r = i // (seq4 // chunk)
        j = i % (seq4 // chunk)
        nat = pl.ds(r + 4 * chunk * j, chunk, stride=4)
        loc = pl.ds(pl.multiple_of(i * chunk, chunk), chunk)
        x1, x4, x16 = m1[nat, :], m4[loc, :], m16[loc, :]
        top = jnp.maximum(jnp.maximum(x1, x4), x16)
        e1, e4, e16 = jnp.exp(x1 - top), jnp.exp(x4 - top), jnp.exp(x16 - top)
        num = e1 * a1[nat, :] + e4 * a4[loc, :] + e16 * a16[loc, :]
        den = e1 * d1[nat, :] + e4 * d4[loc, :] + e16 * d16[loc, :]
        out[nat, :] = num / den
        return c

    lax.fori_loop(0, seq // chunk, mix, 0)
    o_ref[...] = out[...].astype(BF16)


def _attention(proj3):
    B, S, _ = proj3.shape
    H = ATTN_HEADS
    blk = lambda off: pl.BlockSpec((None, S, HEAD_DIM), lambda b, h, off=off: (b, 0, off + h))
    f32buf = pltpu.VMEM((S, HEAD_DIM), F32)
    bf16buf = pltpu.VMEM((S, HEAD_DIM), BF16)
    return pl.pallas_call(
        functools.partial(_attn_kernel, seq=S),
        grid=(B, H),
        in_specs=[blk(0), blk(H), blk(2 * H)],
        out_specs=pl.BlockSpec((None, S, HEAD_DIM), lambda b, h: (b, 0, h)),
        out_shape=jax.ShapeDtypeStruct((B, S, ATTN_WIDTH), BF16),
        scratch_shapes=[f32buf] * 6 + [bf16buf] * 3 + [f32buf] * 6,
        compiler_params=_params("arbitrary", "arbitrary"),
        name="dilated_attention",
    )(proj3, proj3, proj3)


def _ssm_prep_kernel(lr_ref, li_ref, ldt_ref, br_ref, bi_ref, ar_ref, ai_ref, bbr_ref, bbi_ref):
    lr = lr_ref[...]
    li = li_ref[...]
    dt = jnp.exp(ldt_ref[...])
    mag = jnp.exp(lr * dt)
    abar_r = mag * jnp.cos(li * dt)
    abar_i = mag * jnp.sin(li * dt)
    num_r = abar_r - 1.0
    num_i = abar_i
    den = lr * lr + li * li
    coef_r = (num_r * lr + num_i * li) / den
    coef_i = (num_i * lr - num_r * li) / den
    br = br_ref[...]
    bi = bi_ref[...]
    ar_ref[...] = abar_r
    ai_ref[...] = -abar_i
    bbr_ref[...] = coef_r * br - coef_i * bi
    bbi_ref[...] = -(coef_r * bi + coef_i * br)


def _ssm_prep(lam_re, lam_im, log_dt, b_re, b_im):
    G, P, C = SSM_GROUPS, SSM_STATE, SSM_GROUP
    gp = jax.ShapeDtypeStruct((G, 1, P), F32)
    gcp = jax.ShapeDtypeStruct((G, C, P), F32)
    return pl.pallas_call(
        _ssm_prep_kernel,
        out_shape=(gp, gp, gcp, gcp),
        name="ssm_prep",
    )(lam_re.reshape(G, 1, P), lam_im.reshape(G, 1, P), log_dt.reshape(G, 1, 1),
      jnp.swapaxes(b_re, 1, 2), jnp.swapaxes(b_im, 1, 2))


def _block_diag_slabs(m):
    G, a, b = m.shape
    gl = SSM_LANE_BLOCK // SSM_GROUP
    m4 = m.reshape(SSM_SLABS, gl, a, b)
    eye = jnp.eye(gl, dtype=m.dtype)
    out = m4[:, :, :, None, :] * eye[None, :, None, :, None]
    return out.reshape(SSM_SLABS, gl * a, gl * b)


def _chunk_permutation():
    i = np.arange(SSM_TILE)
    j = (i % SSM_CHUNK_LEN) * SSM_CHUNKS + i // SSM_CHUNK_LEN
    perm = np.zeros((SSM_TILE, SSM_TILE), np.float32)
    perm[j, i] = 1.0
    return perm


def _ssm_kernel(u_ref, perm_ref, permt_ref, a_ref, bmat_ref, cmat_ref, d_ref, wglu_ref, bglu_ref,
                gn_ref, o_ref, up_s, bu_s, y_s, car_s):
    L = SSM_CHUNK_LEN
    NS = SLAB_STATES

    @pl.when(pl.program_id(1) == 0)
    def _():
        car_s[...] = jnp.zeros_like(car_s)

    up_s[...] = jnp.dot(perm_ref[...], u_ref[...], preferred_element_type=F32).astype(BF16)
    sub = lax.broadcasted_iota(jnp.int32, (SSM_CHUNKS, NS), 0)

    for kb in range(SSM_SLABS):
        lanes = slice(kb * SSM_LANE_BLOCK, (kb + 1) * SSM_LANE_BLOCK)
        bu_s[...] = jnp.dot(up_s[:, lanes], bmat_ref[kb], preferred_element_type=F32)
        a_r = jnp.broadcast_to(a_ref[kb, 0:1, :], (SSM_CHUNKS, NS))
        a_i = jnp.broadcast_to(a_ref[kb, 1:2, :], (SSM_CHUNKS, NS))

        def step(t, s, store):
            s_r, s_i = s
            rows = pl.ds(pl.multiple_of(t * SSM_CHUNKS, SSM_CHUNKS), SSM_CHUNKS)
            b_r = bu_s[rows, 0:NS]
            b_i = bu_s[rows, NS:2 * NS]
            n_r = a_r * s_r - a_i * s_i + b_r
            n_i = a_r * s_i + a_i * s_r + b_i
            if store:
                bu_s[rows, 0:NS] = n_r
                bu_s[rows, NS:2 * NS] = n_i
            return n_r, n_i

        zero = jnp.zeros((SSM_CHUNKS, NS), F32)
        f_r, f_i = lax.fori_loop(0, L, functools.partial(step, store=False), (zero, zero), unroll=8)

        p_r, p_i = a_ref[kb, 0:1, :], a_ref[kb, 1:2, :]
        for _ in range(int(math.log2(L))):
            p_r, p_i = p_r * p_r - p_i * p_i, 2.0 * p_r * p_i

        g_r, g_i = car_s[kb:kb + 1, 0:NS], car_s[kb:kb + 1, NS:2 * NS]
        i_r, i_i = zero, zero
        for c in range(SSM_CHUNKS):
            i_r = jnp.where(sub == c, jnp.broadcast_to(g_r, (SSM_CHUNKS, NS)), i_r)
            i_i = jnp.where(sub == c, jnp.broadcast_to(g_i, (SSM_CHUNKS, NS)), i_i)
            g_r, g_i = (f_r[c:c + 1] + p_r * g_r - p_i * g_i,
                        f_i[c:c + 1] + p_r * g_i + p_i * g_r)
        car_s[kb:kb + 1, 0:NS] = g_r
        car_s[kb:kb + 1, NS:2 * NS] = g_i

        lax.fori_loop(0, L, functools.partial(step, store=True), (i_r, i_i), unroll=8)
        y_s[:, lanes] = jnp.dot(bu_s[...].astype(BF16), cmat_ref[kb], preferred_element_type=F32)

    y = y_s[...] + d_ref[...] * up_s[...].astype(F32)
    g = 0.5 * y * (1.0 + jnp.tanh(math.sqrt(2.0 / math.pi) * (y + 0.044715 * (y * y * y))))
    gate = jax.nn.sigmoid(jnp.dot(g.astype(BF16), wglu_ref[...], preferred_element_type=F32)
                          + bglu_ref[...])
    on = _rms(g * gate, gn_ref[...]).astype(BF16)
    o_ref[...] = jnp.dot(permt_ref[...], on, preferred_element_type=F32).astype(BF16)


def _ssm(proj3, a_pack, bmat, cmat, d_skip, wglu, bglu, gnorm):
    B, S, _ = proj3.shape
    ucol = 3 * ATTN_WIDTH // SSM_WIDTH
    perm = _chunk_permutation()
    const = lambda shape: pl.BlockSpec(shape, lambda b, i: (0,) * len(shape))
    return pl.pallas_call(
        _ssm_kernel,
        grid=(B, S // SSM_TILE),
        in_specs=[
            pl.BlockSpec((None, SSM_TILE, SSM_WIDTH), lambda b, i: (b, i, ucol)),
            const((SSM_TILE, SSM_TILE)),
            const((SSM_TILE, SSM_TILE)),
            const((SSM_SLABS, 2, SLAB_STATES)),
            const((SSM_SLABS, SSM_LANE_BLOCK, 2 * SLAB_STATES)),
            const((SSM_SLABS, 2 * SLAB_STATES, SSM_LANE_BLOCK)),
            const((1, SSM_WIDTH)),
            const((SSM_WIDTH, SSM_WIDTH)),
            const((1, SSM_WIDTH)),
            const((1, SSM_WIDTH)),
        ],
        out_specs=pl.BlockSpec((None, SSM_TILE, SSM_WIDTH), lambda b, i: (b, i, 0)),
        out_shape=jax.ShapeDtypeStruct((B, S, SSM_WIDTH), BF16),
        scratch_shapes=[
            pltpu.VMEM((SSM_TILE, SSM_WIDTH), BF16),
            pltpu.VMEM((SSM_TILE, 2 * SLAB_STATES), F32),
            pltpu.VMEM((SSM_TILE, SSM_WIDTH), F32),
            pltpu.VMEM((SSM_SLABS, 2 * SLAB_STATES), F32),
        ],
        compiler_params=_params("arbitrary", "arbitrary"),
        name="s5_scan",
    )(proj3, jnp.asarray(perm, BF16), jnp.asarray(perm.T, BF16), a_pack, bmat, cmat,
      d_skip, wglu, bglu, gnorm)


def _mix_kernel(x_ref, attn_ref, ssm_ref, ga_ref, w_ref, gp_ref, o_ref):
    an = _rms(attn_ref[...].astype(F32), ga_ref[...]).astype(BF16)
    mixed = jnp.concatenate([an, ssm_ref[...]], axis=1)
    y = jnp.dot(mixed, w_ref[...], preferred_element_type=F32)
    o_ref[...] = x_ref[...] + _rms(y, gp_ref[...])


def _mix(x2, attn2, ssm2, g_attn, w_out, g_post, tm=512):
    T, D = x2.shape
    row = lambda w: pl.BlockSpec((tm, w), lambda i: (i, 0))
    const = lambda shape: pl.BlockSpec(shape, lambda i: (0, 0))
    return pl.pallas_call(
        _mix_kernel,
        grid=(T // tm,),
        in_specs=[row(D), row(ATTN_WIDTH), row(SSM_WIDTH), const((1, ATTN_WIDTH)),
                  const((D, D)), const((1, D))],
        out_specs=row(D),
        out_shape=jax.ShapeDtypeStruct((T, D), F32),
        compiler_params=_params("arbitrary"),
        name="mix_out",
    )(x2, attn2, ssm2, g_attn, w_out, g_post)


def _mlp_kernel(h_ref, gpre_ref, wup_ref, wdn_ref, gpost_ref, o_ref, hn_ref, acc_ref):
    j = pl.program_id(1)

    @pl.when(j == 0)
    def _():
        hn_ref[...] = _rms(h_ref[...], gpre_ref[...]).astype(BF16)
        acc_ref[...] = jnp.zeros_like(acc_ref)

    a = jnp.dot(hn_ref[...], wup_ref[...], preferred_element_type=F32)
    a = jnp.square(jnp.maximum(a, 0.0)).astype(BF16)
    acc_ref[...] += jnp.dot(a, wdn_ref[...], preferred_element_type=F32)

    @pl.when(j == pl.num_programs(1) - 1)
    def _():
        o_ref[...] = h_ref[...] + _rms(acc_ref[...], gpost_ref[...])


def _mlp(h, g_pre, w_up, w_down, g_post, tm=512, tf=1024):
    T, D = h.shape
    F = w_up.shape[1]
    return pl.pallas_call(
        _mlp_kernel,
        grid=(T // tm, F // tf),
        in_specs=[
            pl.BlockSpec((tm, D), lambda i, j: (i, 0)),
            pl.BlockSpec((1, D), lambda i, j: (0, 0)),
            pl.BlockSpec((D, tf), lambda i, j: (0, j)),
            pl.BlockSpec((tf, D), lambda i, j: (j, 0)),
            pl.BlockSpec((1, D), lambda i, j: (0, 0)),
        ],
        out_specs=pl.BlockSpec((tm, D), lambda i, j: (i, 0)),
        out_shape=jax.ShapeDtypeStruct((T, D), F32),
        scratch_shapes=[pltpu.VMEM((tm, D), BF16), pltpu.VMEM((tm, D), F32)],
        compiler_params=_params("arbitrary", "arbitrary"),
        name="relu2_mlp",
    )(h, g_pre, w_up, w_down, g_post)


def _ple_kernel(h_ref, p_ref, gpre_ref, wg_ref, wp_ref, gpost_ref, o_ref):
    h = h_ref[...]
    hn = _rms(h, gpre_ref[...]).astype(BF16)
    gate = jax.nn.sigmoid(jnp.dot(hn, wg_ref[...], preferred_element_type=F32))
    e = jnp.dot(p_ref[...].astype(BF16), wp_ref[...], preferred_element_type=F32)
    o_ref[...] = h + _rms(gate * e, gpost_ref[...])


def _ple(h, p2, g_pre, w_gate, w_proj, g_post, tm=512):
    T, D = h.shape
    row = lambda w: pl.BlockSpec((tm, w), lambda i: (i, 0))
    const = lambda shape: pl.BlockSpec(shape, lambda i: (0, 0))
    return pl.pallas_call(
        _ple_kernel,
        grid=(T // tm,),
        in_specs=[row(D), row(PLE_DIM), const((1, D)), const((D, D)), const((PLE_DIM, D)),
                  const((1, D))],
        out_specs=row(D),
        out_shape=jax.ShapeDtypeStruct((T, D), F32),
        compiler_params=_params("arbitrary"),
        name="ple_gate",
    )(h, p2, g_pre, w_gate, w_proj, g_post)


def kernel(x, p, mix_norm_pre, w_in, lam_re, lam_im, log_dt, ssm_b_re, ssm_b_im, ssm_c_re, ssm_c_im, ssm_d, w_glu, b_glu, attn_out_norm, ssm_out_norm, w_out, mix_norm_post, mlp_norm_pre, w_up, w_down, mlp_norm_post, ple_norm_pre, w_ple_gate, w_ple_proj, ple_norm_post):
    B, S, D = x.shape
    T = B * S
    h = x.reshape(T, D)
    for i in range(p.shape[0]):
        vec = lambda a: a[i].reshape(1, -1)
        proj, (w_out_b, w_up_b, w_down_b, w_gate_b) = _in_proj(
            h, vec(mix_norm_pre), w_in[i].astype(BF16),
            [w_out[i], w_up[i], w_down[i], w_ple_gate[i]])
        proj3 = proj.reshape(B, S, -1)
        attn = _attention(proj3)

        a_r, a_i, bb_r, bb_i = _ssm_prep(lam_re[i], lam_im[i], log_dt[i], ssm_b_re[i], ssm_b_im[i])
        a_pack = jnp.stack([a_r.reshape(SSM_SLABS, SLAB_STATES), a_i.reshape(SSM_SLABS, SLAB_STATES)],
                           axis=1)
        bmat = jnp.concatenate([_block_diag_slabs(bb_r), _block_diag_slabs(bb_i)], axis=2).astype(BF16)
        cmat = jnp.concatenate([_block_diag_slabs(jnp.swapaxes(ssm_c_re[i], 1, 2)),
                                _block_diag_slabs(jnp.swapaxes(ssm_c_im[i], 1, 2))], axis=1).astype(BF16)
        ssm = _ssm(proj3, a_pack, bmat, cmat, vec(ssm_d), w_glu[i].astype(BF16), vec(b_glu),
                   vec(ssm_out_norm))

        h = _mix(h, attn.reshape(T, -1), ssm.reshape(T, -1), vec(attn_out_norm),
                 w_out_b, vec(mix_norm_post))
        h = _mlp(h, vec(mlp_norm_pre), w_up_b, w_down_b, vec(mlp_norm_post))
        h = _ple(h, p[i].reshape(T, -1), vec(ple_norm_pre), w_gate_b,
                 w_ple_proj[i].astype(BF16), vec(ple_norm_post))
    return h.reshape(B, S, D)
```

```python
import functools
import math

import numpy as np
import jax
import jax.numpy as jnp
from jax import lax
from jax.experimental import pallas as pl
from jax.experimental.pallas import tpu as pltpu

F32 = jnp.float32
BF16 = jnp.bfloat16

D_MODEL = 2048
HEAD_DIM = 128
ATTN_WIDTH = 1024
ATTN_HEADS = ATTN_WIDTH // HEAD_DIM
SSM_WIDTH = 1024
SSM_GROUP = 16
SSM_GROUPS = SSM_WIDTH // SSM_GROUP
SSM_STATE = 64
DILATIONS = (1, 4, 16)
BLK = 128
ATTN_GROUP = 8
D_FF = 4 * D_MODEL
PLE_DIM = 256
RMS_EPS = 1e-6
NEG_INF = -1e30

V7X_VMEM_BYTES = 64 * 1024 * 1024
VMEM_LIMIT = 52 * 1024 * 1024

SSM_TILE = 512
SSM_CHUNKS = 8
SSM_CHUNK_LEN = SSM_TILE // SSM_CHUNKS
SSM_LANE_BLOCK = 128
SSM_SLABS = SSM_WIDTH // SSM_LANE_BLOCK
SLAB_STATES = (SSM_LANE_BLOCK // SSM_GROUP) * SSM_STATE


def _rms(x, g):
    ms = jnp.mean(x * x, axis=-1, keepdims=True)
    return x * lax.rsqrt(ms + RMS_EPS) * g


def _params(*sem, flags=None):
    return pltpu.CompilerParams(dimension_semantics=sem, vmem_limit_bytes=VMEM_LIMIT, flags=flags)


class _CastJob:
    def __init__(self, weights, layer, steps, step_of):
        self.weights = weights
        self.n = len(weights)
        self.in_specs, self.out_specs, self.out_shape = [], [], []
        for w in weights:
            _, rows, cols = w.shape
            assert rows % steps == 0
            blk = rows // steps
            self.in_specs.append(pl.BlockSpec((None, blk, cols),
                                              lambda *g, layer=layer: (layer, step_of(*g), 0)))
            self.out_specs.append(pl.BlockSpec((blk, cols), lambda *g: (step_of(*g), 0)))
            self.out_shape.append(jax.ShapeDtypeStruct((rows, cols), BF16))

    @staticmethod
    def run(srcs, dsts):
        for src, dst in zip(srcs, dsts):
            dst[...] = src[...].astype(BF16)


def _in_proj_kernel(x_ref, g_ref, w_ref, o_ref, xn_ref):
    j = pl.program_id(1)

    @pl.when(j == 0)
    def _():
        xn_ref[...] = _rms(x_ref[...], g_ref[...]).astype(BF16)

    col_scale = jnp.where(j == 0, 1.0 / math.sqrt(HEAD_DIM), 1.0)
    acc = jnp.dot(xn_ref[...], w_ref[...], preferred_element_type=F32)
    o_ref[...] = (acc * col_scale).astype(BF16)


def _in_proj(x2, g, w_bf16, tm=1024, tn=ATTN_WIDTH):
    T, D = x2.shape
    N = w_bf16.shape[1]
    return pl.pallas_call(
        _in_proj_kernel,
        grid=(T // tm, N // tn),
        in_specs=[
            pl.BlockSpec((tm, D), lambda i, j: (i, 0)),
            pl.BlockSpec((1, D), lambda i, j: (0, 0)),
            pl.BlockSpec((D, tn), lambda i, j: (0, j)),
        ],
        out_specs=pl.BlockSpec((tm, tn), lambda i, j: (i, j)),
        out_shape=jax.ShapeDtypeStruct((T, N), BF16),
        scratch_shapes=[pltpu.VMEM((tm, D), BF16)],
        compiler_params=_params("arbitrary", "arbitrary"),
        name="in_proj",
    )(x2, g, w_bf16)


def _attn_kernel(q_ref, k_ref, v_ref, *refs, seq, n_cast):
    cast_in, o_ref, cast_out = refs[:n_cast], refs[n_cast], refs[n_cast + 1:2 * n_cast + 1]
    _CastJob.run(cast_in, cast_out)
    _attn_body(q_ref, k_ref, v_ref, o_ref, *refs[2 * n_cast + 1:], seq=seq)


def _attn_body(q_ref, k_ref, v_ref, o_ref, qf, kf, vf, q4, k4, v4, qd, kd, vd,
               a1, m1, d1, a4, m4, d4, *, seq):
    row = lax.broadcasted_iota(jnp.int32, (BLK, BLK), 0)
    col = lax.broadcasted_iota(jnp.int32, (BLK, BLK), 1)
    cur_ok = col <= row
    both_ok = jnp.concatenate([col >= row, cur_ok], axis=1)
    ones = jnp.ones((BLK, BLK), BF16)

    def banded_group(base, qs, ks, vs, outs, has_prev, dst):
        acc_o, max_o, den_o = outs
        nb = len(has_prev)
        row0 = [base + b * BLK for b in range(nb)]
        s_all, m_all = [], []
        for b in range(nb):
            q = qs[pl.ds(row0[b], BLK), :]
            if has_prev[b]:
                kk = ks[pl.ds(row0[b] - BLK, 2 * BLK), :]
                mask = both_ok
            else:
                kk = ks[pl.ds(row0[b], BLK), :]
                mask = cur_ok
            s = lax.dot_general(q, kk, (((1,), (1,)), ((), ())), preferred_element_type=F32)
            s = jnp.where(mask, s, NEG_INF)
            if has_prev[b]:
                m = jnp.max(jnp.maximum(s[:, :BLK], s[:, BLK:]), axis=1, keepdims=True)
            else:
                m = jnp.max(s, axis=1, keepdims=True)
            s_all.append(s)
            m_all.append(m)
        p_all = [jnp.exp(s - m).astype(BF16) for s, m in zip(s_all, m_all)]
        for b in range(nb):
            if has_prev[b]:
                vv = vs[pl.ds(row0[b] - BLK, 2 * BLK), :]
                aug = jnp.concatenate([vv, jnp.concatenate([ones, ones], axis=0)], axis=1)
            else:
                vv = vs[pl.ds(row0[b], BLK), :]
                aug = jnp.concatenate([vv, ones], axis=1)
            pv = jnp.dot(p_all[b], aug, preferred_element_type=F32)
            rows = dst(b)
            acc_o[rows, :] = pv[:, :BLK]
            den_o[rows, :] = pv[:, BLK:]
            max_o[rows, :] = jnp.broadcast_to(m_all[b], (BLK, BLK))

    def banded(qs, ks, vs, outs, blocks_per_segment, dst_of=None):
        ngroups = seq // BLK // ATTN_GROUP
        span = ATTN_GROUP * BLK
        if dst_of is None:
            dst_of = lambda g, base: (lambda b: pl.ds(base + b * BLK, BLK))

        def run(g, has_prev):
            base = g * span if isinstance(g, int) else pl.multiple_of(g * span, span)
            banded_group(base, qs, ks, vs, outs, has_prev, dst_of(g, base))

        def loop(lo, has_prev):
            def body(g, c):
                run(g, has_prev)
                return c
            lax.fori_loop(lo, ngroups, body, 0)

        if blocks_per_segment == ngroups * ATTN_GROUP:
            run(0, [False] + [True] * (ATTN_GROUP - 1))
            loop(1, [True] * ATTN_GROUP)
        else:
            assert ATTN_GROUP % blocks_per_segment == 0
            loop(0, [b % blocks_per_segment != 0 for b in range(ATTN_GROUP)])

    seq4 = seq // 4

    banded(q_ref, k_ref, v_ref, (a1, m1, d1), seq // BLK)

    qf[...] = q_ref[...].astype(F32)
    kf[...] = k_ref[...].astype(F32)
    vf[...] = v_ref[...].astype(F32)

    def gather4(r, c):
        src = pl.ds(r, seq4, stride=4)
        dst = pl.ds(pl.multiple_of(r * seq4, seq4), seq4)
        for f32_src, f32_dst, bf16_dst in ((qf, q4, qd), (kf, k4, kd), (vf, v4, vd)):
            x = f32_src[src, :]
            f32_dst[dst, :] = x
            bf16_dst[dst, :] = x.astype(BF16)
        return c

    lax.fori_loop(0, 4, gather4, 0)
    banded(qd, kd, vd, (a4, m4, d4), seq4 // BLK)

    seq16 = seq // 16

    def gather16(r, c):
        src = pl.ds((r % 4) * seq4 + r // 4, seq16, stride=4)
        dst = pl.ds(pl.multiple_of(r * seq16, seq16), seq16)
        qd[dst, :] = q4[src, :].astype(BF16)
        kd[dst, :] = k4[src, :].astype(BF16)
        vd[dst, :] = v4[src, :].astype(BF16)
        return c

    lax.fori_loop(0, 16, gather16, 0)

    blocks16 = seq16 // BLK
    a16, m16, d16 = qf, kf, vf

    def dst16(g, base):
        return lambda b: pl.ds((b // blocks16) * seq4 + g + 4 * BLK * (b % blocks16), BLK, stride=4)

    banded(qd, kd, vd, (a16, m16, d16), blocks16, dst16)

    out = q4
    chunk = 256

    def mix(i, c):
        r = i // (seq4 // chunk)
        j = i % (seq4 // chunk)
        nat = pl.ds(r + 4 * chunk * j, chunk, stride=4)
        loc = pl.ds(pl.multiple_of(i * chunk, chunk), chunk)
        x1, x4, x16 = m1[nat, :], m4[loc, :], m16[loc, :]
        top = jnp.maximum(jnp.maximum(x1, x4), x16)
        e1, e4, e16 = jnp.exp(x1 - top), jnp.exp(x4 - top), jnp.exp(x16 - top)
        num = e1 * a1[nat, :] + e4 * a4[loc, :] + e16 * a16[loc, :]
        den = e1 * d1[nat, :] + e4 * d4[loc, :] + e16 * d16[loc, :]
        out[nat, :] = num / den
        return c

    lax.fori_loop(0, seq // chunk, mix, 0)
    o_ref[...] = out[...].astype(BF16)


def _attention(proj3, cast_weights, layer):
    B, S, _ = proj3.shape
    H = ATTN_HEADS
    blk = lambda off: pl.BlockSpec((None, S, HEAD_DIM), lambda b, h, off=off: (b, 0, off + h))
    f32buf = pltpu.VMEM((S, HEAD_DIM), F32)
    bf16buf = pltpu.VMEM((S, HEAD_DIM), BF16)
    job = _CastJob(cast_weights, layer, B * H, lambda b, h: b * H + h)
    outs = pl.pallas_call(
        functools.partial(_attn_kernel, seq=S, n_cast=job.n),
        grid=(B, H),
        in_specs=[blk(0), blk(H), blk(2 * H)] + job.in_specs,
        out_specs=[pl.BlockSpec((None, S, HEAD_DIM), lambda b, h: (b, 0, h))] + job.out_specs,
        out_shape=[jax.ShapeDtypeStruct((B, S, ATTN_WIDTH), BF16)] + job.out_shape,
        scratch_shapes=[f32buf] * 6 + [bf16buf] * 3 + [f32buf] * 6,
        compiler_params=_params("arbitrary", "arbitrary"),
        name="dilated_attention",
    )(proj3, proj3, proj3, *cast_weights)
    return outs[0], outs[1:]


def _ssm_prep_kernel(lr_ref, li_ref, ldt_ref, br_ref, bi_ref, ar_ref, ai_ref, bbr_ref, bbi_ref):
    lr = lr_ref[...]
    li = li_ref[...]
    dt = jnp.exp(ldt_ref[...])
    mag = jnp.exp(lr * dt)
    abar_r = mag * jnp.cos(li * dt)
    abar_i = mag * jnp.sin(li * dt)
    num_r = abar_r - 1.0
    num_i = abar_i
    den = lr * lr + li * li
    coef_r = (num_r * lr + num_i * li) / den
    coef_i = (num_i * lr - num_r * li) / den
    br = br_ref[...]
    bi = bi_ref[...]
    ar_ref[...] = abar_r
    ai_ref[...] = -abar_i
    bbr_ref[...] = coef_r * br - coef_i * bi
    bbi_ref[...] = -(coef_r * bi + coef_i * br)


def _ssm_prep(lam_re, lam_im, log_dt, b_re, b_im):
    G, P, C = SSM_GROUPS, SSM_STATE, SSM_GROUP
    gp = jax.ShapeDtypeStruct((G, 1, P), F32)
    gcp = jax.ShapeDtypeStruct((G, C, P), F32)
    return pl.pallas_call(
        _ssm_prep_kernel,
        out_shape=(gp, gp, gcp, gcp),
        name="ssm_prep",
    )(lam_re.reshape(G, 1, P), lam_im.reshape(G, 1, P), log_dt.reshape(G, 1, 1),
      jnp.swapaxes(b_re, 1, 2), jnp.swapaxes(b_im, 1, 2))


def _block_diag_slabs(m):
    G, a, b = m.shape
    gl = SSM_LANE_BLOCK // SSM_GROUP
    m4 = m.reshape(SSM_SLABS, gl, a, b)
    eye = jnp.eye(gl, dtype=m.dtype)
    out = m4[:, :, :, None, :] * eye[None, :, None, :, None]
    return out.reshape(SSM_SLABS, gl * a, gl * b)


def _chunk_permutation():
    i = np.arange(SSM_TILE)
    j = (i % SSM_CHUNK_LEN) * SSM_CHUNKS + i // SSM_CHUNK_LEN
    perm = np.zeros((SSM_TILE, SSM_TILE), np.float32)
    perm[j, i] = 1.0
    return perm


def _ssm_kernel(*refs, n_cast):
    n_in = 10
    cast_in, cast_out = refs[n_in:n_in + n_cast], refs[n_in + n_cast + 1:n_in + 2 * n_cast + 1]
    _CastJob.run(cast_in, cast_out)
    _ssm_body(*refs[:n_in], refs[n_in + n_cast], *refs[n_in + 2 * n_cast + 1:])


def _ssm_body(u_ref, perm_ref, permt_ref, a_ref, bmat_ref, cmat_ref, d_ref, wglu_ref, bglu_ref,
              gn_ref, o_ref, up_s, bu_s, y_s, car_s):
    L = SSM_CHUNK_LEN
    NS = SLAB_STATES

    @pl.when(pl.program_id(1) == 0)
    def _():
        car_s[...] = jnp.zeros_like(car_s)

    up_s[...] = jnp.dot(perm_ref[...], u_ref[...], preferred_element_type=F32).astype(BF16)
    sub = lax.broadcasted_iota(jnp.int32, (SSM_CHUNKS, NS), 0)

    for kb in range(SSM_SLABS):
        lanes = slice(kb * SSM_LANE_BLOCK, (kb + 1) * SSM_LANE_BLOCK)
        bu_s[...] = jnp.dot(up_s[:, lanes], bmat_ref[kb], preferred_element_type=F32)
        a_r = jnp.broadcast_to(a_ref[kb, 0:1, :], (SSM_CHUNKS, NS))
        a_i = jnp.broadcast_to(a_ref[kb, 1:2, :], (SSM_CHUNKS, NS))

        def step(t, s, store):
            s_r, s_i = s
            rows = pl.ds(pl.multiple_of(t * SSM_CHUNKS, SSM_CHUNKS), SSM_CHUNKS)
            b_r = bu_s[rows, 0:NS]
            b_i = bu_s[rows, NS:2 * NS]
            n_r = a_r * s_r - a_i * s_i + b_r
            n_i = a_r * s_i + a_i * s_r + b_i
            if store:
                bu_s[rows, 0:NS] = n_r
                bu_s[rows, NS:2 * NS] = n_i
            return n_r, n_i

        zero = jnp.zeros((SSM_CHUNKS, NS), F32)
        f_r, f_i = lax.fori_loop(0, L, functools.partial(step, store=False), (zero, zero), unroll=8)

        p_r, p_i = a_ref[kb, 0:1, :], a_ref[kb, 1:2, :]
        for _ in range(int(math.log2(L))):
            p_r, p_i = p_r * p_r - p_i * p_i, 2.0 * p_r * p_i

        g_r, g_i = car_s[kb:kb + 1, 0:NS], car_s[kb:kb + 1, NS:2 * NS]
        i_r, i_i = zero, zero
        for c in range(SSM_CHUNKS):
            i_r = jnp.where(sub == c, jnp.broadcast_to(g_r, (SSM_CHUNKS, NS)), i_r)
            i_i = jnp.where(sub == c, jnp.broadcast_to(g_i, (SSM_CHUNKS, NS)), i_i)
            g_r, g_i = (f_r[c:c + 1] + p_r * g_r - p_i * g_i,
                        f_i[c:c + 1] + p_r * g_i + p_i * g_r)
        car_s[kb:kb + 1, 0:NS] = g_r
        car_s[kb:kb + 1, NS:2 * NS] = g_i

        lax.fori_loop(0, L, functools.partial(step, store=True), (i_r, i_i), unroll=8)
        y_s[:, lanes] = jnp.dot(bu_s[...].astype(BF16), cmat_ref[kb], preferred_element_type=F32)

    y = y_s[...] + d_ref[...] * up_s[...].astype(F32)
    g = 0.5 * y * (1.0 + jnp.tanh(math.sqrt(2.0 / math.pi) * (y + 0.044715 * (y * y * y))))
    gate = jax.nn.sigmoid(jnp.dot(g.astype(BF16), wglu_ref[...], preferred_element_type=F32)
                          + bglu_ref[...])
    on = _rms(g * gate, gn_ref[...]).astype(BF16)
    o_ref[...] = jnp.dot(permt_ref[...], on, preferred_element_type=F32).astype(BF16)


def _ssm(proj3, a_pack, bmat, cmat, d_skip, wglu, bglu, gnorm, cast_weights, layer):
    B, S, _ = proj3.shape
    ucol = 3 * ATTN_WIDTH // SSM_WIDTH
    perm = _chunk_permutation()
    const = lambda shape: pl.BlockSpec(shape, lambda b, i: (0,) * len(shape))
    tiles = S // SSM_TILE
    job = _CastJob(cast_weights, layer, B * tiles, lambda b, i: b * tiles + i)
    outs = pl.pallas_call(
        functools.partial(_ssm_kernel, n_cast=job.n),
        grid=(B, tiles),
        in_specs=[
            pl.BlockSpec((None, SSM_TILE, SSM_WIDTH), lambda b, i: (b, i, ucol)),
            const((SSM_TILE, SSM_TILE)),
            const((SSM_TILE, SSM_TILE)),
            const((SSM_SLABS, 2, SLAB_STATES)),
            const((SSM_SLABS, SSM_LANE_BLOCK, 2 * SLAB_STATES)),
            const((SSM_SLABS, 2 * SLAB_STATES, SSM_LANE_BLOCK)),
            const((1, SSM_WIDTH)),
            const((SSM_WIDTH, SSM_WIDTH)),
            const((1, SSM_WIDTH)),
            const((1, SSM_WIDTH)),
        ] + job.in_specs,
        out_specs=[pl.BlockSpec((None, SSM_TILE, SSM_WIDTH), lambda b, i: (b, i, 0))] + job.out_specs,
        out_shape=[jax.ShapeDtypeStruct((B, S, SSM_WIDTH), BF16)] + job.out_shape,
        scratch_shapes=[
            pltpu.VMEM((SSM_TILE, SSM_WIDTH), BF16),
            pltpu.VMEM((SSM_TILE, 2 * SLAB_STATES), F32),
            pltpu.VMEM((SSM_TILE, SSM_WIDTH), F32),
            pltpu.VMEM((SSM_SLABS, 2 * SLAB_STATES), F32),
        ],
        compiler_params=_params("arbitrary", "arbitrary"),
        name="s5_scan",
    )(proj3, jnp.asarray(perm, BF16), jnp.asarray(perm.T, BF16), a_pack, bmat, cmat,
      d_skip, wglu, bglu, gnorm, *cast_weights)
    return outs[0], outs[1:]


def _mix_kernel(x_ref, attn_ref, ssm_ref, ga_ref, w_ref, gp_ref, o_ref):
    an = _rms(attn_ref[...].astype(F32), ga_ref[...]).astype(BF16)
    mixed = jnp.concatenate([an, ssm_ref[...]], axis=1)
    y = jnp.dot(mixed, w_ref[...], preferred_element_type=F32)
    o_ref[...] = x_ref[...] + _rms(y, gp_ref[...])


def _mix(x2, attn2, ssm2, g_attn, w_out, g_post, tm=512):
    T, D = x2.shape
    row = lambda w: pl.BlockSpec((tm, w), lambda i: (i, 0))
    const = lambda shape: pl.BlockSpec(shape, lambda i: (0, 0))
    return pl.pallas_call(
        _mix_kernel,
        grid=(T // tm,),
        in_specs=[row(D), row(ATTN_WIDTH), row(SSM_WIDTH), const((1, ATTN_WIDTH)),
                  const((D, D)), const((1, D))],
        out_specs=row(D),
        out_shape=jax.ShapeDtypeStruct((T, D), F32),
        compiler_params=_params("arbitrary"),
        name="mix_out",
    )(x2, attn2, ssm2, g_attn, w_out, g_post)


def _mlp_kernel(h_ref, gpre_ref, wup_ref, wdn_ref, gpost_ref, o_ref, hn_ref, acc_ref):
    j = pl.program_id(1)

    @pl.when(j == 0)
    def _():
        hn_ref[...] = _rms(h_ref[...], gpre_ref[...]).astype(BF16)
        acc_ref[...] = jnp.zeros_like(acc_ref)

    a = jnp.dot(hn_ref[...], wup_ref[...], preferred_element_type=F32)
    a = jnp.square(jnp.maximum(a, 0.0)).astype(BF16)
    acc_ref[...] += jnp.dot(a, wdn_ref[...], preferred_element_type=F32)

    @pl.when(j == pl.num_programs(1) - 1)
    def _():
        o_ref[...] = h_ref[...] + _rms(acc_ref[...], gpost_ref[...])


def _mlp(h, g_pre, w_up, w_down, g_post, tm=512, tf=1024):
    T, D = h.shape
    F = w_up.shape[1]
    return pl.pallas_call(
        _mlp_kernel,
        grid=(T // tm, F // tf),
        in_specs=[
            pl.BlockSpec((tm, D), lambda i, j: (i, 0)),
            pl.BlockSpec((1, D), lambda i, j: (0, 0)),
            pl.BlockSpec((D, tf), lambda i, j: (0, j)),
            pl.BlockSpec((tf, D), lambda i, j: (j, 0)),
            pl.BlockSpec((1, D), lambda i, j: (0, 0)),
        ],
        out_specs=pl.BlockSpec((tm, D), lambda i, j: (i, 0)),
        out_shape=jax.ShapeDtypeStruct((T, D), F32),
        scratch_shapes=[pltpu.VMEM((tm, D), BF16), pltpu.VMEM((tm, D), F32)],
        compiler_params=_params("arbitrary", "arbitrary"),
        name="relu2_mlp",
    )(h, g_pre, w_up, w_down, g_post)


def _ple_kernel(h_ref, p_ref, gpre_ref, wg_ref, wp_ref, gpost_ref, o_ref):
    h = h_ref[...]
    hn = _rms(h, gpre_ref[...]).astype(BF16)
    gate = jax.nn.sigmoid(jnp.dot(hn, wg_ref[...], preferred_element_type=F32))
    e = jnp.dot(p_ref[...].astype(BF16), wp_ref[...], preferred_element_type=F32)
    o_ref[...] = h + _rms(gate * e, gpost_ref[...])


def _ple(h, p2, g_pre, w_gate, w_proj, g_post, tm=512):
    T, D = h.shape
    row = lambda w: pl.BlockSpec((tm, w), lambda i: (i, 0))
    const = lambda shape: pl.BlockSpec(shape, lambda i: (0, 0))
    return pl.pallas_call(
        _ple_kernel,
        grid=(T // tm,),
        in_specs=[row(D), row(PLE_DIM), const((1, D)), const((D, D)), const((PLE_DIM, D)),
                  const((1, D))],
        out_specs=row(D),
        out_shape=jax.ShapeDtypeStruct((T, D), F32),
        compiler_params=_params("arbitrary"),
        name="ple_gate",
    )(h, p2, g_pre, w_gate, w_proj, g_post)


def kernel(x, p, mix_norm_pre, w_in, lam_re, lam_im, log_dt, ssm_b_re, ssm_b_im, ssm_c_re, ssm_c_im, ssm_d, w_glu, b_glu, attn_out_norm, ssm_out_norm, w_out, mix_norm_post, mlp_norm_pre, w_up, w_down, mlp_norm_post, ple_norm_pre, w_ple_gate, w_ple_proj, ple_norm_post):
    B, S, D = x.shape
    T = B * S
    h = x.reshape(T, D)
    for i in range(p.shape[0]):
        vec = lambda a: a[i].reshape(1, -1)
        proj = _in_proj(h, vec(mix_norm_pre), w_in[i].astype(BF16))
        proj3 = proj.reshape(B, S, -1)
        attn, (w_out_b, w_gate_b) = _attention(proj3, [w_out, w_ple_gate], i)

        a_r, a_i, bb_r, bb_i = _ssm_prep(lam_re[i], lam_im[i], log_dt[i], ssm_b_re[i], ssm_b_im[i])
        a_pack = jnp.stack([a_r.reshape(SSM_SLABS, SLAB_STATES), a_i.reshape(SSM_SLABS, SLAB_STATES)],
                           axis=1)
        bmat = jnp.concatenate([_block_diag_slabs(bb_r), _block_diag_slabs(bb_i)], axis=2).astype(BF16)
        cmat = jnp.concatenate([_block_diag_slabs(jnp.swapaxes(ssm_c_re[i], 1, 2)),
                                _block_diag_slabs(jnp.swapaxes(ssm_c_im[i], 1, 2))], axis=1).astype(BF16)
        ssm, (w_up_b, w_down_b) = _ssm(proj3, a_pack, bmat, cmat, vec(ssm_d), w_glu[i].astype(BF16),
                                       vec(b_glu), vec(ssm_out_norm), [w_up, w_down], i)

        h = _mix(h, attn.reshape(T, -1), ssm.reshape(T, -1), vec(attn_out_norm),
                 w_out_b, vec(mix_norm_post))
        h = _mlp(h, vec(mlp_norm_pre), w_up_b, w_down_b, vec(mlp_norm_post))
        h = _ple(h, p[i].reshape(T, -1), vec(ple_norm_pre), w_gate_b,
                 w_ple_proj[i].astype(BF16), vec(ple_norm_post))
    return h.reshape(B, S, D)
```

```python
import functools
import math

import numpy as np
import jax
import jax.numpy as jnp
from jax import lax
from jax.experimental import pallas as pl
from jax.experimental.pallas import tpu as pltpu

F32 = jnp.float32
BF16 = jnp.bfloat16

D_MODEL = 2048
HEAD_DIM = 128
ATTN_WIDTH = 1024
ATTN_HEADS = ATTN_WIDTH // HEAD_DIM
SSM_WIDTH = 1024
SSM_GROUP = 16
SSM_GROUPS = SSM_WIDTH // SSM_GROUP
SSM_STATE = 64
DILATIONS = (1, 4, 16)
BLK = 128
ATTN_GROUP = 8
D_FF = 4 * D_MODEL
PLE_DIM = 256
RMS_EPS = 1e-6
NEG_INF = -1e30

V7X_VMEM_BYTES = 64 * 1024 * 1024
VMEM_LIMIT = 52 * 1024 * 1024

SSM_TILE = 512
SSM_CHUNKS = 8
SSM_CHUNK_LEN = SSM_TILE // SSM_CHUNKS
SSM_LANE_BLOCK = 128
SSM_SLABS = SSM_WIDTH // SSM_LANE_BLOCK
SLAB_STATES = (SSM_LANE_BLOCK // SSM_GROUP) * SSM_STATE


def _rms(x, g):
    ms = jnp.mean(x * x, axis=-1, keepdims=True)
    return x * lax.rsqrt(ms + RMS_EPS) * g


def _params(*sem, flags=None):
    return pltpu.CompilerParams(dimension_semantics=sem, vmem_limit_bytes=VMEM_LIMIT, flags=flags)


class _CastJob:
    def __init__(self, weights, layer, steps, step_of):
        self.weights = weights
        self.n = len(weights)
        self.in_specs, self.out_specs, self.out_shape = [], [], []
        for w in weights:
            _, rows, cols = w.shape
            assert rows % steps == 0
            blk = rows // steps
            self.in_specs.append(pl.BlockSpec((None, blk, cols),
                                              lambda *g, layer=layer: (layer, step_of(*g), 0)))
            self.out_specs.append(pl.BlockSpec((blk, cols), lambda *g: (step_of(*g), 0)))
            self.out_shape.append(jax.ShapeDtypeStruct((rows, cols), BF16))

    @staticmethod
    def run(srcs, dsts):
        for src, dst in zip(srcs, dsts):
            dst[...] = src[...].astype(BF16)


def _in_proj_kernel(x_ref, g_ref, w_ref, o_ref, xn_ref):
    j = pl.program_id(1)

    @pl.when(j == 0)
    def _():
        xn_ref[...] = _rms(x_ref[...], g_ref[...]).astype(BF16)

    col_scale = jnp.where(j == 0, 1.0 / math.sqrt(HEAD_DIM), 1.0)
    acc = jnp.dot(xn_ref[...], w_ref[...], preferred_element_type=F32)
    o_ref[...] = (acc * col_scale).astype(BF16)


def _in_proj(x2, g, w_bf16, tm=1024, tn=ATTN_WIDTH):
    T, D = x2.shape
    N = w_bf16.shape[1]
    return pl.pallas_call(
        _in_proj_kernel,
        grid=(T // tm, N // tn),
        in_specs=[
            pl.BlockSpec((tm, D), lambda i, j: (i, 0)),
            pl.BlockSpec((1, D), lambda i, j: (0, 0)),
            pl.BlockSpec((D, tn), lambda i, j: (0, j)),
        ],
        out_specs=pl.BlockSpec((tm, tn), lambda i, j: (i, j)),
        out_shape=jax.ShapeDtypeStruct((T, N), BF16),
        scratch_shapes=[pltpu.VMEM((tm, D), BF16)],
        compiler_params=_params("arbitrary", "arbitrary"),
        name="in_proj",
    )(x2, g, w_bf16)


def _attn_kernel(q_ref, k_ref, v_ref, *refs, seq, n_cast):
    cast_in, o_ref, cast_out = refs[:n_cast], refs[n_cast], refs[n_cast + 1:2 * n_cast + 1]
    _CastJob.run(cast_in, cast_out)
    _attn_body(q_ref, k_ref, v_ref, o_ref, *refs[2 * n_cast + 1:], seq=seq)


def _attn_body(q_ref, k_ref, v_ref, o_ref, qf, kf, vf, q4, k4, v4, qd, kd, vd,
               a1, m1, d1, a4, m4, d4, *, seq):
    row = lax.broadcasted_iota(jnp.int32, (BLK, BLK), 0)
    col = lax.broadcasted_iota(jnp.int32, (BLK, BLK), 1)
    cur_ok = col <= row
    both_ok = jnp.concatenate([col >= row, cur_ok], axis=1)
    ones = jnp.ones((BLK, BLK), BF16)

    def banded_group(base, qs, ks, vs, outs, has_prev, dst):
        acc_o, max_o, den_o = outs
        nb = len(has_prev)
        row0 = [base + b * BLK for b in range(nb)]
        s_all, m_all = [], []
        for b in range(nb):
            q = qs[pl.ds(row0[b], BLK), :]
            if has_prev[b]:
                kk = ks[pl.ds(row0[b] - BLK, 2 * BLK), :]
                mask = both_ok
            else:
                kk = ks[pl.ds(row0[b], BLK), :]
                mask = cur_ok
            s = lax.dot_general(q, kk, (((1,), (1,)), ((), ())), preferred_element_type=F32)
            s = jnp.where(mask, s, NEG_INF)
            if has_prev[b]:
                m = jnp.max(jnp.maximum(s[:, :BLK], s[:, BLK:]), axis=1, keepdims=True)
            else:
                m = jnp.max(s, axis=1, keepdims=True)
            s_all.append(s)
            m_all.append(m)
        p_all = [jnp.exp(s - m).astype(BF16) for s, m in zip(s_all, m_all)]
        for b in range(nb):
            if has_prev[b]:
                vv = vs[pl.ds(row0[b] - BLK, 2 * BLK), :]
                aug = jnp.concatenate([vv, jnp.concatenate([ones, ones], axis=0)], axis=1)
            else:
                vv = vs[pl.ds(row0[b], BLK), :]
                aug = jnp.concatenate([vv, ones], axis=1)
            pv = jnp.dot(p_all[b], aug, preferred_element_type=F32)
            rows = dst(b)
            acc_o[rows, :] = pv[:, :BLK]
            den_o[rows, :] = pv[:, BLK:]
            max_o[rows, :] = jnp.broadcast_to(m_all[b], (BLK, BLK))

    def banded(qs, ks, vs, outs, blocks_per_segment, dst_of=None):
        ngroups = seq // BLK // ATTN_GROUP
        span = ATTN_GROUP * BLK
        if dst_of is None:
            dst_of = lambda g, base: (lambda b: pl.ds(base + b * BLK, BLK))

        def run(g, has_prev):
            base = g * span if isinstance(g, int) else pl.multiple_of(g * span, span)
            banded_group(base, qs, ks, vs, outs, has_prev, dst_of(g, base))

        def loop(lo, has_prev):
            def body(g, c):
                run(g, has_prev)
                return c
            lax.fori_loop(lo, ngroups, body, 0)

        if blocks_per_segment == ngroups * ATTN_GROUP:
            run(0, [False] + [True] * (ATTN_GROUP - 1))
            loop(1, [True] * ATTN_GROUP)
        else:
            assert ATTN_GROUP % blocks_per_segment == 0
            loop(0, [b % blocks_per_segment != 0 for b in range(ATTN_GROUP)])

    seq4 = seq // 4

    banded(q_ref, k_ref, v_ref, (a1, m1, d1), seq // BLK)

    qf[...] = q_ref[...].astype(F32)
    kf[...] = k_ref[...].astype(F32)
    vf[...] = v_ref[...].astype(F32)

    def gather4(r, c):
        src = pl.ds(r, seq4, stride=4)
        dst = pl.ds(pl.multiple_of(r * seq4, seq4), seq4)
        for f32_src, f32_dst, bf16_dst in ((qf, q4, qd), (kf, k4, kd), (vf, v4, vd)):
            x = f32_src[src, :]
            f32_dst[dst, :] = x
            bf16_dst[dst, :] = x.astype(BF16)
        return c

    lax.fori_loop(0, 4, gather4, 0)
    banded(qd, kd, vd, (a4, m4, d4), seq4 // BLK)

    seq16 = seq // 16

    def gather16(r, c):
        src = pl.ds((r % 4) * seq4 + r // 4, seq16, stride=4)
        dst = pl.ds(pl.multiple_of(r * seq16, seq16), seq16)
        qd[dst, :] = q4[src, :].astype(BF16)
        kd[dst, :] = k4[src, :].astype(BF16)
        vd[dst, :] = v4[src, :].astype(BF16)
        return c

    lax.fori_loop(0, 16, gather16, 0)

    blocks16 = seq16 // BLK
    a16, m16, d16 = qf, kf, vf

    def dst16(g, base):
        return lambda b: pl.ds((b // blocks16) * seq4 + g + 4 * BLK * (b % blocks16), BLK, stride=4)

    banded(qd, kd, vd, (a16, m16, d16), blocks16, dst16)

    out = q4
    chunk = 256

    def mix(i, c):
        r = i // (seq4 // chunk)
        j = i % (seq4 // chunk)
        nat = pl.ds(r + 4 * chunk * j, chunk, stride=4)
        loc = pl.ds(pl.multiple_of(i * chunk, chunk), chunk)
        x1, x4, x16 = m1[nat, :], m4[loc, :], m16[loc, :]
        top = jnp.maximum(jnp.maximum(x1, x4), x16)
        e1, e4, e16 = jnp.exp(x1 - top), jnp.exp(x4 - top), jnp.exp(x16 - top)
        num = e1 * a1[nat, :] + e4 * a4[loc, :] + e16 * a16[loc, :]
        den = e1 * d1[nat, :] + e4 * d4[loc, :] + e16 * d16[loc, :]
        out[nat, :] = num / den
        return c

    lax.fori_loop(0, seq // chunk, mix, 0)
    o_ref[...] = out[...].astype(BF16)


def _attention(proj3, cast_weights, layer):
    B, S, _ = proj3.shape
    H = ATTN_HEADS
    blk = lambda off: pl.BlockSpec((None, S, HEAD_DIM), lambda b, h, off=off: (b, 0, off + h))
    f32buf = pltpu.VMEM((S, HEAD_DIM), F32)
    bf16buf = pltpu.VMEM((S, HEAD_DIM), BF16)
    job = _CastJob(cast_weights, layer, B * H, lambda b, h: b * H + h)
    outs = pl.pallas_call(
        functools.partial(_attn_kernel, seq=S, n_cast=job.n),
        grid=(B, H),
        in_specs=[blk(0), blk(H), blk(2 * H)] + job.in_specs,
        out_specs=[pl.BlockSpec((None, S, HEAD_DIM), lambda b, h: (b, 0, h))] + job.out_specs,
        out_shape=[jax.ShapeDtypeStruct((B, S, ATTN_WIDTH), BF16)] + job.out_shape,
        scratch_shapes=[f32buf] * 6 + [bf16buf] * 3 + [f32buf] * 6,
        compiler_params=_params("arbitrary", "arbitrary"),
        name="dilated_attention",
    )(proj3, proj3, proj3, *cast_weights)
    return outs[0], outs[1:]


def _ssm_prep_kernel(lr_ref, li_ref, ldt_ref, br_ref, bi_ref, ar_ref, ai_ref, bbr_ref, bbi_ref):
    lr = lr_ref[...]
    li = li_ref[...]
    dt = jnp.exp(ldt_ref[...])
    mag = jnp.exp(lr * dt)
    abar_r = mag * jnp.cos(li * dt)
    abar_i = mag * jnp.sin(li * dt)
    num_r = abar_r - 1.0
    num_i = abar_i
    den = lr * lr + li * li
    coef_r = (num_r * lr + num_i * li) / den
    coef_i = (num_i * lr - num_r * li) / den
    br = br_ref[...]
    bi = bi_ref[...]
    ar_ref[...] = abar_r
    ai_ref[...] = -abar_i
    bbr_ref[...] = coef_r * br - coef_i * bi
    bbi_ref[...] = -(coef_r * bi + coef_i * br)


def _ssm_prep(lam_re, lam_im, log_dt, b_re, b_im):
    G, P, C = SSM_GROUPS, SSM_STATE, SSM_GROUP
    gp = jax.ShapeDtypeStruct((G, 1, P), F32)
    gcp = jax.ShapeDtypeStruct((G, C, P), F32)
    return pl.pallas_call(
        _ssm_prep_kernel,
        out_shape=(gp, gp, gcp, gcp),
        name="ssm_prep",
    )(lam_re.reshape(G, 1, P), lam_im.reshape(G, 1, P), log_dt.reshape(G, 1, 1),
      jnp.swapaxes(b_re, 1, 2), jnp.swapaxes(b_im, 1, 2))


def _block_diag_slabs(m):
    G, a, b = m.shape
    gl = SSM_LANE_BLOCK // SSM_GROUP
    m4 = m.reshape(SSM_SLABS, gl, a, b)
    eye = jnp.eye(gl, dtype=m.dtype)
    out = m4[:, :, :, None, :] * eye[None, :, None, :, None]
    return out.reshape(SSM_SLABS, gl * a, gl * b)


def _chunk_permutation():
    i = np.arange(SSM_TILE)
    j = (i % SSM_CHUNK_LEN) * SSM_CHUNKS + i // SSM_CHUNK_LEN
    perm = np.zeros((SSM_TILE, SSM_TILE), np.float32)
    perm[j, i] = 1.0
    return perm


def _ssm_kernel(*refs, n_cast):
    n_in = 10
    cast_in, cast_out = refs[n_in:n_in + n_cast], refs[n_in + n_cast + 1:n_in + 2 * n_cast + 1]
    _CastJob.run(cast_in, cast_out)
    _ssm_body(*refs[:n_in], refs[n_in + n_cast], *refs[n_in + 2 * n_cast + 1:])


def _ssm_body(u_ref, perm_ref, permt_ref, a_ref, bmat_ref, cmat_ref, d_ref, wglu_ref, bglu_ref,
              gn_ref, o_ref, up_s, bu_s, y_s, car_s):
    L = SSM_CHUNK_LEN
    NS = SLAB_STATES

    @pl.when(pl.program_id(1) == 0)
    def _():
        car_s[...] = jnp.zeros_like(car_s)

    up_s[...] = jnp.dot(perm_ref[...], u_ref[...], preferred_element_type=F32).astype(BF16)
    sub = lax.broadcasted_iota(jnp.int32, (SSM_CHUNKS, NS), 0)

    def lanes_of(kb):
        return slice(kb * SSM_LANE_BLOCK, (kb + 1) * SSM_LANE_BLOCK)

    nbuf = bu_s.shape[0]
    zero = jnp.zeros((SSM_CHUNKS, NS), F32)

    def project_in(kb):
        bu_s[kb % nbuf] = jnp.dot(up_s[:, lanes_of(kb)], bmat_ref[kb], preferred_element_type=F32)

    def scan(slabs, init, store):
        state = list(init)
        for t in range(L):
            rows = slice(t * SSM_CHUNKS, (t + 1) * SSM_CHUNKS)
            for k, kb in enumerate(slabs):
                buf = bu_s.at[kb % nbuf]
                a_r, a_i = coef[kb]
                s_r, s_i = state[k]
                s_r, s_i = (a_r * s_r - a_i * s_i + buf[rows, 0:NS],
                            a_r * s_i + a_i * s_r + buf[rows, NS:2 * NS])
                if store:
                    buf[rows, 0:NS] = s_r
                    buf[rows, NS:2 * NS] = s_i
                state[k] = (s_r, s_i)
        return state

    def chunk_inits(kb, final):
        f_r, f_i = final
        p_r, p_i = a_ref[kb, 0:1, :], a_ref[kb, 1:2, :]
        for _ in range(int(math.log2(L))):
            p_r, p_i = p_r * p_r - p_i * p_i, 2.0 * p_r * p_i
        g_r, g_i = car_s[kb:kb + 1, 0:NS], car_s[kb:kb + 1, NS:2 * NS]
        i_r, i_i = zero, zero
        for c in range(SSM_CHUNKS):
            i_r = jnp.where(sub == c, jnp.broadcast_to(g_r, (SSM_CHUNKS, NS)), i_r)
            i_i = jnp.where(sub == c, jnp.broadcast_to(g_i, (SSM_CHUNKS, NS)), i_i)
            g_r, g_i = (f_r[c:c + 1] + p_r * g_r - p_i * g_i,
                        f_i[c:c + 1] + p_r * g_i + p_i * g_r)
        car_s[kb:kb + 1, 0:NS] = g_r
        car_s[kb:kb + 1, NS:2 * NS] = g_i
        return i_r, i_i

    coef = {}
    together = nbuf // 2
    groups = [list(range(g, g + together)) for g in range(0, SSM_SLABS, together)]
    for kb in groups[0]:
        project_in(kb)
    for gi, slabs in enumerate(groups):
        if gi + 1 < len(groups):
            for kb in groups[gi + 1]:
                project_in(kb)
        for kb in slabs:
            coef[kb] = (jnp.broadcast_to(a_ref[kb, 0:1, :], (SSM_CHUNKS, NS)),
                        jnp.broadcast_to(a_ref[kb, 1:2, :], (SSM_CHUNKS, NS)))
        finals = scan(slabs, [(zero, zero)] * len(slabs), False)
        scan(slabs, [chunk_inits(kb, f) for kb, f in zip(slabs, finals)], True)
        for kb in slabs:
            y_s[:, lanes_of(kb)] = jnp.dot(bu_s[kb % nbuf].astype(BF16), cmat_ref[kb],
                                           preferred_element_type=F32)

    y = y_s[...] + d_ref[...] * up_s[...].astype(F32)
    g = 0.5 * y * (1.0 + jnp.tanh(math.sqrt(2.0 / math.pi) * (y + 0.044715 * (y * y * y))))
    gate = jax.nn.sigmoid(jnp.dot(g.astype(BF16), wglu_ref[...], preferred_element_type=F32)
                          + bglu_ref[...])
    on = _rms(g * gate, gn_ref[...]).astype(BF16)
    o_ref[...] = jnp.dot(permt_ref[...], on, preferred_element_type=F32).astype(BF16)


def _ssm(proj3, a_pack, bmat, cmat, d_skip, wglu, bglu, gnorm, cast_weights, layer):
    B, S, _ = proj3.shape
    ucol = 3 * ATTN_WIDTH // SSM_WIDTH
    perm = _chunk_permutation()
    const = lambda shape: pl.BlockSpec(shape, lambda b, i: (0,) * len(shape))
    tiles = S // SSM_TILE
    job = _CastJob(cast_weights, layer, B * tiles, lambda b, i: b * tiles + i)
    outs = pl.pallas_call(
        functools.partial(_ssm_kernel, n_cast=job.n),
        grid=(B, tiles),
        in_specs=[
            pl.BlockSpec((None, SSM_TILE, SSM_WIDTH), lambda b, i: (b, i, ucol)),
            const((SSM_TILE, SSM_TILE)),
            const((SSM_TILE, SSM_TILE)),
            const((SSM_SLABS, 2, SLAB_STATES)),
            const((SSM_SLABS, SSM_LANE_BLOCK, 2 * SLAB_STATES)),
            const((SSM_SLABS, 2 * SLAB_STATES, SSM_LANE_BLOCK)),
            const((1, SSM_WIDTH)),
            const((SSM_WIDTH, SSM_WIDTH)),
            const((1, SSM_WIDTH)),
            const((1, SSM_WIDTH)),
        ] + job.in_specs,
        out_specs=[pl.BlockSpec((None, SSM_TILE, SSM_WIDTH), lambda b, i: (b, i, 0))] + job.out_specs,
        out_shape=[jax.ShapeDtypeStruct((B, S, SSM_WIDTH), BF16)] + job.out_shape,
        scratch_shapes=[
            pltpu.VMEM((SSM_TILE, SSM_WIDTH), BF16),
            pltpu.VMEM((4, SSM_TILE, 2 * SLAB_STATES), F32),
            pltpu.VMEM((SSM_TILE, SSM_WIDTH), F32),
            pltpu.VMEM((SSM_SLABS, 2 * SLAB_STATES), F32),
        ],
        compiler_params=_params("arbitrary", "arbitrary"),
        name="s5_scan",
    )(proj3, jnp.asarray(perm, BF16), jnp.asarray(perm.T, BF16), a_pack, bmat, cmat,
      d_skip, wglu, bglu, gnorm, *cast_weights)
    return outs[0], outs[1:]


def _mix_kernel(x_ref, attn_ref, ssm_ref, ga_ref, w_ref, gp_ref, o_ref):
    an = _rms(attn_ref[...].astype(F32), ga_ref[...]).astype(BF16)
    mixed = jnp.concatenate([an, ssm_ref[...]], axis=1)
    y = jnp.dot(mixed, w_ref[...], preferred_element_type=F32)
    o_ref[...] = x_ref[...] + _rms(y, gp_ref[...])


def _mix(x2, attn2, ssm2, g_attn, w_out, g_post, tm=512):
    T, D = x2.shape
    row = lambda w: pl.BlockSpec((tm, w), lambda i: (i, 0))
    const = lambda shape: pl.BlockSpec(shape, lambda i: (0, 0))
    return pl.pallas_call(
        _mix_kernel,
        grid=(T // tm,),
        in_specs=[row(D), row(ATTN_WIDTH), row(SSM_WIDTH), const((1, ATTN_WIDTH)),
                  const((D, D)), const((1, D))],
        out_specs=row(D),
        out_shape=jax.ShapeDtypeStruct((T, D), F32),
        compiler_params=_params("arbitrary"),
        name="mix_out",
    )(x2, attn2, ssm2, g_attn, w_out, g_post)


def _mlp_kernel(h_ref, gpre_ref, wup_ref, wdn_ref, gpost_ref, o_ref, hn_ref, acc_ref):
    j = pl.program_id(1)

    @pl.when(j == 0)
    def _():
        hn_ref[...] = _rms(h_ref[...], gpre_ref[...]).astype(BF16)
        acc_ref[...] = jnp.zeros_like(acc_ref)

    a = jnp.dot(hn_ref[...], wup_ref[...], preferred_element_type=F32)
    a = jnp.square(jnp.maximum(a, 0.0)).astype(BF16)
    acc_ref[...] += jnp.dot(a, wdn_ref[...], preferred_element_type=F32)

    @pl.when(j == pl.num_programs(1) - 1)
    def _():
        o_ref[...] = h_ref[...] + _rms(acc_ref[...], gpost_ref[...])


def _mlp(h, g_pre, w_up, w_down, g_post, tm=512, tf=1024):
    T, D = h.shape
    F = w_up.shape[1]
    return pl.pallas_call(
        _mlp_kernel,
        grid=(T // tm, F // tf),
        in_specs=[
            pl.BlockSpec((tm, D), lambda i, j: (i, 0)),
            pl.BlockSpec((1, D), lambda i, j: (0, 0)),
            pl.BlockSpec((D, tf), lambda i, j: (0, j)),
            pl.BlockSpec((tf, D), lambda i, j: (j, 0)),
            pl.BlockSpec((1, D), lambda i, j: (0, 0)),
        ],
        out_specs=pl.BlockSpec((tm, D), lambda i, j: (i, 0)),
        out_shape=jax.ShapeDtypeStruct((T, D), F32),
        scratch_shapes=[pltpu.VMEM((tm, D), BF16), pltpu.VMEM((tm, D), F32)],
        compiler_params=_params("arbitrary", "arbitrary"),
        name="relu2_mlp",
    )(h, g_pre, w_up, w_down, g_post)


def _ple_kernel(h_ref, p_ref, gpre_ref, wg_ref, wp_ref, gpost_ref, o_ref):
    h = h_ref[...]
    hn = _rms(h, gpre_ref[...]).astype(BF16)
    gate = jax.nn.sigmoid(jnp.dot(hn, wg_ref[...], preferred_element_type=F32))
    e = jnp.dot(p_ref[...].astype(BF16), wp_ref[...], preferred_element_type=F32)
    o_ref[...] = h + _rms(gate * e, gpost_ref[...])


def _ple(h, p2, g_pre, w_gate, w_proj, g_post, tm=512):
    T, D = h.shape
    row = lambda w: pl.BlockSpec((tm, w), lambda i: (i, 0))
    const = lambda shape: pl.BlockSpec(shape, lambda i: (0, 0))
    return pl.pallas_call(
        _ple_kernel,
        grid=(T // tm,),
        in_specs=[row(D), row(PLE_DIM), const((1, D)), const((D, D)), const((PLE_DIM, D)),
                  const((1, D))],
        out_specs=row(D),
        out_shape=jax.ShapeDtypeStruct((T, D), F32),
        compiler_params=_params("arbitrary"),
        name="ple_gate",
    )(h, p2, g_pre, w_gate, w_proj, g_post)


def kernel(x, p, mix_norm_pre, w_in, lam_re, lam_im, log_dt, ssm_b_re, ssm_b_im, ssm_c_re, ssm_c_im, ssm_d, w_glu, b_glu, attn_out_norm, ssm_out_norm, w_out, mix_norm_post, mlp_norm_pre, w_up, w_down, mlp_norm_post, ple_norm_pre, w_ple_gate, w_ple_proj, ple_norm_post):
    B, S, D = x.shape
    T = B * S
    h = x.reshape(T, D)
    for i in range(p.shape[0]):
        vec = lambda a: a[i].reshape(1, -1)
        proj = _in_proj(h, vec(mix_norm_pre), w_in[i].astype(BF16))
        proj3 = proj.reshape(B, S, -1)
        attn, (w_out_b, w_gate_b) = _attention(proj3, [w_out, w_ple_gate], i)

        a_r, a_i, bb_r, bb_i = _ssm_prep(lam_re[i], lam_im[i], log_dt[i], ssm_b_re[i], ssm_b_im[i])
        a_pack = jnp.stack([a_r.reshape(SSM_SLABS, SLAB_STATES), a_i.reshape(SSM_SLABS, SLAB_STATES)],
                           axis=1)
        bmat = jnp.concatenate([_block_diag_slabs(bb_r), _block_diag_slabs(bb_i)], axis=2).astype(BF16)
        cmat = jnp.concatenate([_block_diag_slabs(jnp.swapaxes(ssm_c_re[i], 1, 2)),
                                _block_diag_slabs(jnp.swapaxes(ssm_c_im[i], 1, 2))], axis=1).astype(BF16)
        ssm, (w_up_b, w_down_b) = _ssm(proj3, a_pack, bmat, cmat, vec(ssm_d), w_glu[i].astype(BF16),
                                       vec(b_glu), vec(ssm_out_norm), [w_up, w_down], i)

        h = _mix(h, attn.reshape(T, -1), ssm.reshape(T, -1), vec(attn_out_norm),
                 w_out_b, vec(mix_norm_post))
        h = _mlp(h, vec(mlp_norm_pre), w_up_b, w_down_b, vec(mlp_norm_post))
        h = _ple(h, p[i].reshape(T, -1), vec(ple_norm_pre), w_gate_b,
                 w_ple_proj[i].astype(BF16), vec(ple_norm_post))
    return h.reshape(B, S, D)
```

```python
import functools
import math

import numpy as np
import jax
import jax.numpy as jnp
from jax import lax
from jax.experimental import pallas as pl
from jax.experimental.pallas import tpu as pltpu

F32 = jnp.float32
BF16 = jnp.bfloat16

D_MODEL = 2048
HEAD_DIM = 128
ATTN_WIDTH = 1024
ATTN_HEADS = ATTN_WIDTH // HEAD_DIM
SSM_WIDTH = 1024
SSM_GROUP = 16
SSM_GROUPS = SSM_WIDTH // SSM_GROUP
SSM_STATE = 64
DILATIONS = (1, 4, 16)
BLK = 128
ATTN_GROUP = 8
D_FF = 4 * D_MODEL
PLE_DIM = 256
RMS_EPS = 1e-6
NEG_INF = -1e30

V7X_VMEM_BYTES = 64 * 1024 * 1024
VMEM_LIMIT = 52 * 1024 * 1024

SSM_TILE = 512
SSM_CHUNKS = 8
SSM_CHUNK_LEN = SSM_TILE // SSM_CHUNKS
SSM_LANE_BLOCK = 128
SSM_SLABS = SSM_WIDTH // SSM_LANE_BLOCK
SLAB_STATES = (SSM_LANE_BLOCK // SSM_GROUP) * SSM_STATE


def _rms(x, g):
    ms = jnp.mean(x * x, axis=-1, keepdims=True)
    return x * lax.rsqrt(ms + RMS_EPS) * g


def _params(*sem, flags=None):
    return pltpu.CompilerParams(dimension_semantics=sem, vmem_limit_bytes=VMEM_LIMIT, flags=flags)


class _CastJob:
    def __init__(self, weights, layer, steps, step_of):
        self.weights = weights
        self.n = len(weights)
        self.in_specs, self.out_specs, self.out_shape = [], [], []
        for w in weights:
            _, rows, cols = w.shape
            assert rows % steps == 0
            blk = rows // steps
            self.in_specs.append(pl.BlockSpec((None, blk, cols),
                                              lambda *g, layer=layer: (layer, step_of(*g), 0)))
            self.out_specs.append(pl.BlockSpec((blk, cols), lambda *g: (step_of(*g), 0)))
            self.out_shape.append(jax.ShapeDtypeStruct((rows, cols), BF16))

    @staticmethod
    def run(srcs, dsts):
        for src, dst in zip(srcs, dsts):
            dst[...] = src[...].astype(BF16)


def _in_proj_kernel(x_ref, g_ref, w_ref, o_ref, xn_ref):
    j = pl.program_id(1)

    @pl.when(j == 0)
    def _():
        xn_ref[...] = _rms(x_ref[...], g_ref[...]).astype(BF16)

    col_scale = jnp.where(j == 0, 1.0 / math.sqrt(HEAD_DIM), 1.0)
    acc = jnp.dot(xn_ref[...], w_ref[...], preferred_element_type=F32)
    o_ref[...] = (acc * col_scale).astype(BF16)


def _in_proj(x2, g, w_bf16, tm=1024, tn=ATTN_WIDTH):
    T, D = x2.shape
    N = w_bf16.shape[1]
    return pl.pallas_call(
        _in_proj_kernel,
        grid=(T // tm, N // tn),
        in_specs=[
            pl.BlockSpec((tm, D), lambda i, j: (i, 0)),
            pl.BlockSpec((1, D), lambda i, j: (0, 0)),
            pl.BlockSpec((D, tn), lambda i, j: (0, j)),
        ],
        out_specs=pl.BlockSpec((tm, tn), lambda i, j: (i, j)),
        out_shape=jax.ShapeDtypeStruct((T, N), BF16),
        scratch_shapes=[pltpu.VMEM((tm, D), BF16)],
        compiler_params=_params("arbitrary", "arbitrary"),
        name="in_proj",
    )(x2, g, w_bf16)


def _attn_kernel(q_ref, k_ref, v_ref, *refs, seq, n_cast):
    cast_in, o_ref, cast_out = refs[:n_cast], refs[n_cast], refs[n_cast + 1:2 * n_cast + 1]
    _CastJob.run(cast_in, cast_out)
    _attn_body(q_ref, k_ref, v_ref, o_ref, *refs[2 * n_cast + 1:], seq=seq)


def _attn_body(q_ref, k_ref, v_ref, o_ref, qf, kf, vf, q4, k4, v4, qd, kd, vd, qe, ke, ve,
               a1, m1, d1, a4, m4, d4, *, seq):
    row = lax.broadcasted_iota(jnp.int32, (BLK, BLK), 0)
    col = lax.broadcasted_iota(jnp.int32, (BLK, BLK), 1)
    cur_ok = col <= row
    both_ok = jnp.concatenate([col >= row, cur_ok], axis=1)
    ones = jnp.ones((BLK, BLK), BF16)

    def banded_group(base, qs, ks, vs, outs, has_prev, dst):
        acc_o, max_o, den_o = outs
        nb = len(has_prev)
        row0 = [base + b * BLK for b in range(nb)]
        s_all, m_all = [], []
        for b in range(nb):
            q = qs[pl.ds(row0[b], BLK), :]
            if has_prev[b]:
                kk = ks[pl.ds(row0[b] - BLK, 2 * BLK), :]
                mask = both_ok
            else:
                kk = ks[pl.ds(row0[b], BLK), :]
                mask = cur_ok
            s = lax.dot_general(q, kk, (((1,), (1,)), ((), ())), preferred_element_type=F32)
            s = jnp.where(mask, s, NEG_INF)
            if has_prev[b]:
                m = jnp.max(jnp.maximum(s[:, :BLK], s[:, BLK:]), axis=1, keepdims=True)
            else:
                m = jnp.max(s, axis=1, keepdims=True)
            s_all.append(s)
            m_all.append(m)
        p_all = [jnp.exp(s - m).astype(BF16) for s, m in zip(s_all, m_all)]
        for b in range(nb):
            if has_prev[b]:
                vv = vs[pl.ds(row0[b] - BLK, 2 * BLK), :]
                aug = jnp.concatenate([vv, jnp.concatenate([ones, ones], axis=0)], axis=1)
            else:
                vv = vs[pl.ds(row0[b], BLK), :]
                aug = jnp.concatenate([vv, ones], axis=1)
            pv = jnp.dot(p_all[b], aug, preferred_element_type=F32)
            rows = dst(b)
            acc_o[rows, :] = pv[:, :BLK]
            den_o[rows, :] = pv[:, BLK:]
            max_o[rows, :] = jnp.broadcast_to(m_all[b], (BLK, BLK))

    ngroups = seq // BLK // ATTN_GROUP

    def banded(qs, ks, vs, outs, blocks_per_segment, dst_of=None, between=None):
        span = ATTN_GROUP * BLK
        if dst_of is None:
            dst_of = lambda g, base: (lambda b: pl.ds(base + b * BLK, BLK))
        for g in range(ngroups):
            if blocks_per_segment == ngroups * ATTN_GROUP:
                has_prev = [g > 0] + [True] * (ATTN_GROUP - 1)
            else:
                assert ATTN_GROUP % blocks_per_segment == 0
                has_prev = [b % blocks_per_segment != 0 for b in range(ATTN_GROUP)]
            banded_group(g * span, qs, ks, vs, outs, has_prev, dst_of(g, g * span))
            if between is not None:
                between(g)

    seq4 = seq // 4
    seq16 = seq // 16

    qf[...] = q_ref[...].astype(F32)
    kf[...] = k_ref[...].astype(F32)
    vf[...] = v_ref[...].astype(F32)

    def gather4(r):
        src = pl.ds(r, seq4, stride=4)
        dst = pl.ds(r * seq4, seq4)
        for f32_src, f32_dst, bf16_dst in ((qf, q4, qd), (kf, k4, kd), (vf, v4, vd)):
            x = f32_src[src, :]
            f32_dst[dst, :] = x
            bf16_dst[dst, :] = x.astype(BF16)

    def gather16(r):
        src = pl.ds((r % 4) * seq4 + r // 4, seq16, stride=4)
        dst = pl.ds(r * seq16, seq16)
        qe[dst, :] = q4[src, :].astype(BF16)
        ke[dst, :] = k4[src, :].astype(BF16)
        ve[dst, :] = v4[src, :].astype(BF16)

    def gather4_piece(g):
        for r in range(g * 4 // ngroups, (g + 1) * 4 // ngroups):
            gather4(r)

    def gather16_piece(g):
        for r in range(g * 16 // ngroups, (g + 1) * 16 // ngroups):
            gather16(r)

    banded(q_ref, k_ref, v_ref, (a1, m1, d1), seq // BLK, between=gather4_piece)
    banded(qd, kd, vd, (a4, m4, d4), seq4 // BLK, between=gather16_piece)


    blocks16 = seq16 // BLK
    a16, m16, d16 = qf, kf, vf

    def dst16(g, base):
        return lambda b: pl.ds((b // blocks16) * seq4 + g + 4 * BLK * (b % blocks16), BLK, stride=4)

    banded(qe, ke, ve, (a16, m16, d16), blocks16, dst16)

    out = q4
    chunk = 256

    for i in range(seq // chunk):
        r = i // (seq4 // chunk)
        j = i % (seq4 // chunk)
        nat = pl.ds(r + 4 * chunk * j, chunk, stride=4)
        loc = pl.ds(i * chunk, chunk)
        x1, x4, x16 = m1[nat, :], m4[loc, :], m16[loc, :]
        top = jnp.maximum(jnp.maximum(x1, x4), x16)
        e1, e4, e16 = jnp.exp(x1 - top), jnp.exp(x4 - top), jnp.exp(x16 - top)
        num = e1 * a1[nat, :] + e4 * a4[loc, :] + e16 * a16[loc, :]
        den = e1 * d1[nat, :] + e4 * d4[loc, :] + e16 * d16[loc, :]
        out[nat, :] = num / den
    o_ref[...] = out[...].astype(BF16)


def _attention(proj3, cast_weights, layer):
    B, S, _ = proj3.shape
    H = ATTN_HEADS
    blk = lambda off: pl.BlockSpec((None, S, HEAD_DIM), lambda b, h, off=off: (b, 0, off + h))
    f32buf = pltpu.VMEM((S, HEAD_DIM), F32)
    bf16buf = pltpu.VMEM((S, HEAD_DIM), BF16)
    job = _CastJob(cast_weights, layer, B * H, lambda b, h: b * H + h)
    outs = pl.pallas_call(
        functools.partial(_attn_kernel, seq=S, n_cast=job.n),
        grid=(B, H),
        in_specs=[blk(0), blk(H), blk(2 * H)] + job.in_specs,
        out_specs=[pl.BlockSpec((None, S, HEAD_DIM), lambda b, h: (b, 0, h))] + job.out_specs,
        out_shape=[jax.ShapeDtypeStruct((B, S, ATTN_WIDTH), BF16)] + job.out_shape,
        scratch_shapes=[f32buf] * 6 + [bf16buf] * 6 + [f32buf] * 6,
        compiler_params=_params("arbitrary", "arbitrary"),
        name="dilated_attention",
    )(proj3, proj3, proj3, *cast_weights)
    return outs[0], outs[1:]


def _ssm_prep_kernel(lr_ref, li_ref, ldt_ref, br_ref, bi_ref, ar_ref, ai_ref, bbr_ref, bbi_ref):
    lr = lr_ref[...]
    li = li_ref[...]
    dt = jnp.exp(ldt_ref[...])
    mag = jnp.exp(lr * dt)
    abar_r = mag * jnp.cos(li * dt)
    abar_i = mag * jnp.sin(li * dt)
    num_r = abar_r - 1.0
    num_i = abar_i
    den = lr * lr + li * li
    coef_r = (num_r * lr + num_i * li) / den
    coef_i = (num_i * lr - num_r * li) / den
    br = br_ref[...]
    bi = bi_ref[...]
    ar_ref[...] = abar_r
    ai_ref[...] = -abar_i
    bbr_ref[...] = coef_r * br - coef_i * bi
    bbi_ref[...] = -(coef_r * bi + coef_i * br)


def _ssm_prep(lam_re, lam_im, log_dt, b_re, b_im):
    G, P, C = SSM_GROUPS, SSM_STATE, SSM_GROUP
    gp = jax.ShapeDtypeStruct((G, 1, P), F32)
    gcp = jax.ShapeDtypeStruct((G, C, P), F32)
    return pl.pallas_call(
        _ssm_prep_kernel,
        out_shape=(gp, gp, gcp, gcp),
        name="ssm_prep",
    )(lam_re.reshape(G, 1, P), lam_im.reshape(G, 1, P), log_dt.reshape(G, 1, 1),
      jnp.swapaxes(b_re, 1, 2), jnp.swapaxes(b_im, 1, 2))


def _block_diag_slabs(m):
    G, a, b = m.shape
    gl = SSM_LANE_BLOCK // SSM_GROUP
    m4 = m.reshape(SSM_SLABS, gl, a, b)
    eye = jnp.eye(gl, dtype=m.dtype)
    out = m4[:, :, :, None, :] * eye[None, :, None, :, None]
    return out.reshape(SSM_SLABS, gl * a, gl * b)


def _chunk_permutation():
    i = np.arange(SSM_TILE)
    j = (i % SSM_CHUNK_LEN) * SSM_CHUNKS + i // SSM_CHUNK_LEN
    perm = np.zeros((SSM_TILE, SSM_TILE), np.float32)
    perm[j, i] = 1.0
    return perm


def _ssm_kernel(*refs, n_cast):
    n_in = 10
    cast_in, cast_out = refs[n_in:n_in + n_cast], refs[n_in + n_cast + 1:n_in + 2 * n_cast + 1]
    _CastJob.run(cast_in, cast_out)
    _ssm_body(*refs[:n_in], refs[n_in + n_cast], *refs[n_in + 2 * n_cast + 1:])


def _ssm_body(u_ref, perm_ref, permt_ref, a_ref, bmat_ref, cmat_ref, d_ref, wglu_ref, bglu_ref,
              gn_ref, o_ref, up_s, bu_s, y_s, car_s):
    L = SSM_CHUNK_LEN
    NS = SLAB_STATES

    @pl.when(pl.program_id(1) == 0)
    def _():
        car_s[...] = jnp.zeros_like(car_s)

    up_s[...] = jnp.dot(perm_ref[...], u_ref[...], preferred_element_type=F32).astype(BF16)
    sub = lax.broadcasted_iota(jnp.int32, (SSM_CHUNKS, NS), 0)

    def lanes_of(kb):
        return slice(kb * SSM_LANE_BLOCK, (kb + 1) * SSM_LANE_BLOCK)

    nbuf = bu_s.shape[0]
    zero = jnp.zeros((SSM_CHUNKS, NS), F32)

    def project_in(kb):
        bu_s[kb % nbuf] = jnp.dot(up_s[:, lanes_of(kb)], bmat_ref[kb], preferred_element_type=F32)

    def scan(slabs, init, store):
        state = list(init)
        for t in range(L):
            rows = slice(t * SSM_CHUNKS, (t + 1) * SSM_CHUNKS)
            for k, kb in enumerate(slabs):
                buf = bu_s.at[kb % nbuf]
                a_r, a_i = coef[kb]
                s_r, s_i = state[k]
                s_r, s_i = (a_r * s_r - a_i * s_i + buf[rows, 0:NS],
                            a_r * s_i + a_i * s_r + buf[rows, NS:2 * NS])
                if store:
                    buf[rows, 0:NS] = s_r
                    buf[rows, NS:2 * NS] = s_i
                state[k] = (s_r, s_i)
        return state

    def chunk_inits(kb, final):
        f_r, f_i = final
        p_r, p_i = a_ref[kb, 0:1, :], a_ref[kb, 1:2, :]
        for _ in range(int(math.log2(L))):
            p_r, p_i = p_r * p_r - p_i * p_i, 2.0 * p_r * p_i
        g_r, g_i = car_s[kb:kb + 1, 0:NS], car_s[kb:kb + 1, NS:2 * NS]
        i_r, i_i = zero, zero
        for c in range(SSM_CHUNKS):
            i_r = jnp.where(sub == c, jnp.broadcast_to(g_r, (SSM_CHUNKS, NS)), i_r)
            i_i = jnp.where(sub == c, jnp.broadcast_to(g_i, (SSM_CHUNKS, NS)), i_i)
            g_r, g_i = (f_r[c:c + 1] + p_r * g_r - p_i * g_i,
                        f_i[c:c + 1] + p_r * g_i + p_i * g_r)
        car_s[kb:kb + 1, 0:NS] = g_r
        car_s[kb:kb + 1, NS:2 * NS] = g_i
        return i_r, i_i

    coef = {}
    together = nbuf // 2
    groups = [list(range(g, g + together)) for g in range(0, SSM_SLABS, together)]
    for kb in groups[0]:
        project_in(kb)
    for gi, slabs in enumerate(groups):
        if gi + 1 < len(groups):
            for kb in groups[gi + 1]:
                project_in(kb)
        for kb in slabs:
            coef[kb] = (jnp.broadcast_to(a_ref[kb, 0:1, :], (SSM_CHUNKS, NS)),
                        jnp.broadcast_to(a_ref[kb, 1:2, :], (SSM_CHUNKS, NS)))
        finals = scan(slabs, [(zero, zero)] * len(slabs), False)
        scan(slabs, [chunk_inits(kb, f) for kb, f in zip(slabs, finals)], True)
        for kb in slabs:
            y_s[:, lanes_of(kb)] = jnp.dot(bu_s[kb % nbuf].astype(BF16), cmat_ref[kb],
                                           preferred_element_type=F32)

    y = y_s[...] + d_ref[...] * up_s[...].astype(F32)
    g = 0.5 * y * (1.0 + jnp.tanh(math.sqrt(2.0 / math.pi) * (y + 0.044715 * (y * y * y))))
    gate = jax.nn.sigmoid(jnp.dot(g.astype(BF16), wglu_ref[...], preferred_element_type=F32)
                          + bglu_ref[...])
    on = _rms(g * gate, gn_ref[...]).astype(BF16)
    o_ref[...] = jnp.dot(permt_ref[...], on, preferred_element_type=F32).astype(BF16)


def _ssm(proj3, a_pack, bmat, cmat, d_skip, wglu, bglu, gnorm, cast_weights, layer):
    B, S, _ = proj3.shape
    ucol = 3 * ATTN_WIDTH // SSM_WIDTH
    perm = _chunk_permutation()
    const = lambda shape: pl.BlockSpec(shape, lambda b, i: (0,) * len(shape))
    tiles = S // SSM_TILE
    job = _CastJob(cast_weights, layer, B * tiles, lambda b, i: b * tiles + i)
    outs = pl.pallas_call(
        functools.partial(_ssm_kernel, n_cast=job.n),
        grid=(B, tiles),
        in_specs=[
            pl.BlockSpec((None, SSM_TILE, SSM_WIDTH), lambda b, i: (b, i, ucol)),
            const((SSM_TILE, SSM_TILE)),
            const((SSM_TILE, SSM_TILE)),
            const((SSM_SLABS, 2, SLAB_STATES)),
            const((SSM_SLABS, SSM_LANE_BLOCK, 2 * SLAB_STATES)),
            const((SSM_SLABS, 2 * SLAB_STATES, SSM_LANE_BLOCK)),
            const((1, SSM_WIDTH)),
            const((SSM_WIDTH, SSM_WIDTH)),
            const((1, SSM_WIDTH)),
            const((1, SSM_WIDTH)),
        ] + job.in_specs,
        out_specs=[pl.BlockSpec((None, SSM_TILE, SSM_WIDTH), lambda b, i: (b, i, 0))] + job.out_specs,
        out_shape=[jax.ShapeDtypeStruct((B, S, SSM_WIDTH), BF16)] + job.out_shape,
        scratch_shapes=[
            pltpu.VMEM((SSM_TILE, SSM_WIDTH), BF16),
            pltpu.VMEM((4, SSM_TILE, 2 * SLAB_STATES), F32),
            pltpu.VMEM((SSM_TILE, SSM_WIDTH), F32),
            pltpu.VMEM((SSM_SLABS, 2 * SLAB_STATES), F32),
        ],
        compiler_params=_params("arbitrary", "arbitrary"),
        name="s5_scan",
    )(proj3, jnp.asarray(perm, BF16), jnp.asarray(perm.T, BF16), a_pack, bmat, cmat,
      d_skip, wglu, bglu, gnorm, *cast_weights)
    return outs[0], outs[1:]


def _mix_kernel(x_ref, attn_ref, ssm_ref, ga_ref, w_ref, gp_ref, o_ref):
    an = _rms(attn_ref[...].astype(F32), ga_ref[...]).astype(BF16)
    mixed = jnp.concatenate([an, ssm_ref[...]], axis=1)
    y = jnp.dot(mixed, w_ref[...], preferred_element_type=F32)
    o_ref[...] = x_ref[...] + _rms(y, gp_ref[...])


def _mix(x2, attn2, ssm2, g_attn, w_out, g_post, tm=512):
    T, D = x2.shape
    row = lambda w: pl.BlockSpec((tm, w), lambda i: (i, 0))
    const = lambda shape: pl.BlockSpec(shape, lambda i: (0, 0))
    return pl.pallas_call(
        _mix_kernel,
        grid=(T // tm,),
        in_specs=[row(D), row(ATTN_WIDTH), row(SSM_WIDTH), const((1, ATTN_WIDTH)),
                  const((D, D)), const((1, D))],
        out_specs=row(D),
        out_shape=jax.ShapeDtypeStruct((T, D), F32),
        compiler_params=_params("arbitrary"),
        name="mix_out",
    )(x2, attn2, ssm2, g_attn, w_out, g_post)


def _mlp_kernel(h_ref, gpre_ref, wup_ref, wdn_ref, gpost_ref, o_ref, hn_ref, acc_ref):
    j = pl.program_id(1)

    @pl.when(j == 0)
    def _():
        hn_ref[...] = _rms(h_ref[...], gpre_ref[...]).astype(BF16)
        acc_ref[...] = jnp.zeros_like(acc_ref)

    a = jnp.dot(hn_ref[...], wup_ref[...], preferred_element_type=F32)
    a = jnp.square(jnp.maximum(a, 0.0)).astype(BF16)
    acc_ref[...] += jnp.dot(a, wdn_ref[...], preferred_element_type=F32)

    @pl.when(j == pl.num_programs(1) - 1)
    def _():
        o_ref[...] = h_ref[...] + _rms(acc_ref[...], gpost_ref[...])


def _mlp(h, g_pre, w_up, w_down, g_post, tm=512, tf=1024):
    T, D = h.shape
    F = w_up.shape[1]
    return pl.pallas_call(
        _mlp_kernel,
        grid=(T // tm, F // tf),
        in_specs=[
            pl.BlockSpec((tm, D), lambda i, j: (i, 0)),
            pl.BlockSpec((1, D), lambda i, j: (0, 0)),
            pl.BlockSpec((D, tf), lambda i, j: (0, j)),
            pl.BlockSpec((tf, D), lambda i, j: (j, 0)),
            pl.BlockSpec((1, D), lambda i, j: (0, 0)),
        ],
        out_specs=pl.BlockSpec((tm, D), lambda i, j: (i, 0)),
        out_shape=jax.ShapeDtypeStruct((T, D), F32),
        scratch_shapes=[pltpu.VMEM((tm, D), BF16), pltpu.VMEM((tm, D), F32)],
        compiler_params=_params("arbitrary", "arbitrary"),
        name="relu2_mlp",
    )(h, g_pre, w_up, w_down, g_post)


def _ple_kernel(h_ref, p_ref, gpre_ref, wg_ref, wp_ref, gpost_ref, o_ref):
    h = h_ref[...]
    hn = _rms(h, gpre_ref[...]).astype(BF16)
    gate = jax.nn.sigmoid(jnp.dot(hn, wg_ref[...], preferred_element_type=F32))
    e = jnp.dot(p_ref[...].astype(BF16), wp_ref[...], preferred_element_type=F32)
    o_ref[...] = h + _rms(gate * e, gpost_ref[...])


def _ple(h, p2, g_pre, w_gate, w_proj, g_post, tm=512):
    T, D = h.shape
    row = lambda w: pl.BlockSpec((tm, w), lambda i: (i, 0))
    const = lambda shape: pl.BlockSpec(shape, lambda i: (0, 0))
    return pl.pallas_call(
        _ple_kernel,
        grid=(T // tm,),
        in_specs=[row(D), row(PLE_DIM), const((1, D)), const((D, D)), const((PLE_DIM, D)),
                  const((1, D))],
        out_specs=row(D),
        out_shape=jax.ShapeDtypeStruct((T, D), F32),
        compiler_params=_params("arbitrary"),
        name="ple_gate",
    )(h, p2, g_pre, w_gate, w_proj, g_post)


def kernel(x, p, mix_norm_pre, w_in, lam_re, lam_im, log_dt, ssm_b_re, ssm_b_im, ssm_c_re, ssm_c_im, ssm_d, w_glu, b_glu, attn_out_norm, ssm_out_norm, w_out, mix_norm_post, mlp_norm_pre, w_up, w_down, mlp_norm_post, ple_norm_pre, w_ple_gate, w_ple_proj, ple_norm_post):
    B, S, D = x.shape
    T = B * S
    h = x.reshape(T, D)
    for i in range(p.shape[0]):
        vec = lambda a: a[i].reshape(1, -1)
        proj = _in_proj(h, vec(mix_norm_pre), w_in[i].astype(BF16))
        proj3 = proj.reshape(B, S, -1)
        attn, (w_out_b, w_gate_b) = _attention(proj3, [w_out, w_ple_gate], i)

        a_r, a_i, bb_r, bb_i = _ssm_prep(lam_re[i], lam_im[i], log_dt[i], ssm_b_re[i], ssm_b_im[i])
        a_pack = jnp.stack([a_r.reshape(SSM_SLABS, SLAB_STATES), a_i.reshape(SSM_SLABS, SLAB_STATES)],
                           axis=1)
        bmat = jnp.concatenate([_block_diag_slabs(bb_r), _block_diag_slabs(bb_i)], axis=2).astype(BF16)
        cmat = jnp.concatenate([_block_diag_slabs(jnp.swapaxes(ssm_c_re[i], 1, 2)),
                                _block_diag_slabs(jnp.swapaxes(ssm_c_im[i], 1, 2))], axis=1).astype(BF16)
        ssm, (w_up_b, w_down_b) = _ssm(proj3, a_pack, bmat, cmat, vec(ssm_d), w_glu[i].astype(BF16),
                                       vec(b_glu), vec(ssm_out_norm), [w_up, w_down], i)

        h = _mix(h, attn.reshape(T, -1), ssm.reshape(T, -1), vec(attn_out_norm),
                 w_out_b, vec(mix_norm_post))
        h = _mlp(h, vec(mlp_norm_pre), w_up_b, w_down_b, vec(mlp_norm_post))
        h = _ple(h, p[i].reshape(T, -1), vec(ple_norm_pre), w_gate_b,
                 w_ple_proj[i].astype(BF16), vec(ple_norm_post))
    return h.reshape(B, S, D)
```

```python
import functools
import math

import numpy as np
import jax
import jax.numpy as jnp
from jax import lax
from jax.experimental import pallas as pl
from jax.experimental.pallas import tpu as pltpu

F32 = jnp.float32
BF16 = jnp.bfloat16

D_MODEL = 2048
HEAD_DIM = 128
ATTN_WIDTH = 1024
ATTN_HEADS = ATTN_WIDTH // HEAD_DIM
SSM_WIDTH = 1024
SSM_GROUP = 16
SSM_GROUPS = SSM_WIDTH // SSM_GROUP
SSM_STATE = 64
DILATIONS = (1, 4, 16)
BLK = 128
ATTN_GROUP = 8
D_FF = 4 * D_MODEL
PLE_DIM = 256
RMS_EPS = 1e-6
NEG_INF = -1e30

V7X_VMEM_BYTES = 64 * 1024 * 1024
VMEM_LIMIT = 52 * 1024 * 1024
MLP_VMEM_LIMIT = 58 * 1024 * 1024

SSM_TILE = 512
SSM_CHUNKS = 8
SSM_CHUNK_LEN = SSM_TILE // SSM_CHUNKS
SSM_LANE_BLOCK = 128
SSM_SLABS = SSM_WIDTH // SSM_LANE_BLOCK
SLAB_STATES = (SSM_LANE_BLOCK // SSM_GROUP) * SSM_STATE


def _rms(x, g):
    ms = jnp.mean(x * x, axis=-1, keepdims=True)
    return x * lax.rsqrt(ms + RMS_EPS) * g


def _params(*sem, vmem_limit=VMEM_LIMIT):
    return pltpu.CompilerParams(dimension_semantics=sem, vmem_limit_bytes=vmem_limit)


class _CastJob:
    def __init__(self, weights, layer, steps, step_of):
        self.weights = weights
        self.n = len(weights)
        self.in_specs, self.out_specs, self.out_shape = [], [], []
        for w in weights:
            _, rows, cols = w.shape
            assert rows % steps == 0
            blk = rows // steps
            self.in_specs.append(pl.BlockSpec((None, blk, cols),
                                              lambda *g, layer=layer: (layer, step_of(*g), 0)))
            self.out_specs.append(pl.BlockSpec((blk, cols), lambda *g: (step_of(*g), 0)))
            self.out_shape.append(jax.ShapeDtypeStruct((rows, cols), BF16))

    @staticmethod
    def run(srcs, dsts):
        for src, dst in zip(srcs, dsts):
            dst[...] = src[...].astype(BF16)


def _in_proj_kernel(x_ref, g_ref, w_ref, o_ref, xn_ref):
    j = pl.program_id(1)

    @pl.when(j == 0)
    def _():
        xn_ref[...] = _rms(x_ref[...], g_ref[...]).astype(BF16)

    col_scale = jnp.where(j == 0, 1.0 / math.sqrt(HEAD_DIM), 1.0)
    acc = jnp.dot(xn_ref[...], w_ref[...], preferred_element_type=F32)
    o_ref[...] = (acc * col_scale).astype(BF16)


def _in_proj(x2, g, w_bf16, tm=1024, tn=ATTN_WIDTH):
    T, D = x2.shape
    N = w_bf16.shape[1]
    return pl.pallas_call(
        _in_proj_kernel,
        grid=(T // tm, N // tn),
        in_specs=[
            pl.BlockSpec((tm, D), lambda i, j: (i, 0)),
            pl.BlockSpec((1, D), lambda i, j: (0, 0)),
            pl.BlockSpec((D, tn), lambda i, j: (0, j)),
        ],
        out_specs=pl.BlockSpec((tm, tn), lambda i, j: (i, j)),
        out_shape=jax.ShapeDtypeStruct((T, N), BF16),
        scratch_shapes=[pltpu.VMEM((tm, D), BF16)],
        compiler_params=_params("arbitrary", "arbitrary"),
        name="in_proj",
    )(x2, g, w_bf16)


def _attn_kernel(q_ref, k_ref, v_ref, *refs, seq, n_cast):
    cast_in, o_ref, cast_out = refs[:n_cast], refs[n_cast], refs[n_cast + 1:2 * n_cast + 1]
    _CastJob.run(cast_in, cast_out)
    _attn_body(q_ref, k_ref, v_ref, o_ref, *refs[2 * n_cast + 1:], seq=seq)


def _attn_body(q_ref, k_ref, v_ref, o_ref, qf, kf, vf, q4, k4, v4, qd, kd, vd, qe, ke, ve,
               a1, m1, d1, a4, m4, d4, *, seq):
    row = lax.broadcasted_iota(jnp.int32, (BLK, BLK), 0)
    col = lax.broadcasted_iota(jnp.int32, (BLK, BLK), 1)
    cur_ok = col <= row
    both_ok = jnp.concatenate([col >= row, cur_ok], axis=1)
    ones = jnp.ones((BLK, BLK), BF16)

    def banded_group(base, qs, ks, vs, outs, has_prev, dst):
        acc_o, max_o, den_o = outs
        nb = len(has_prev)
        row0 = [base + b * BLK for b in range(nb)]
        s_all, m_all = [], []
        for b in range(nb):
            q = qs[pl.ds(row0[b], BLK), :]
            if has_prev[b]:
                kk = ks[pl.ds(row0[b] - BLK, 2 * BLK), :]
                mask = both_ok
            else:
                kk = ks[pl.ds(row0[b], BLK), :]
                mask = cur_ok
            s = lax.dot_general(q, kk, (((1,), (1,)), ((), ())), preferred_element_type=F32)
            s = jnp.where(mask, s, NEG_INF)
            if has_prev[b]:
                m = jnp.max(jnp.maximum(s[:, :BLK], s[:, BLK:]), axis=1, keepdims=True)
            else:
                m = jnp.max(s, axis=1, keepdims=True)
            s_all.append(s)
            m_all.append(m)
        p_all = [jnp.exp(s - m).astype(BF16) for s, m in zip(s_all, m_all)]
        for b in range(nb):
            if has_prev[b]:
                vv = vs[pl.ds(row0[b] - BLK, 2 * BLK), :]
                aug = jnp.concatenate([vv, jnp.concatenate([ones, ones], axis=0)], axis=1)
            else:
                vv = vs[pl.ds(row0[b], BLK), :]
                aug = jnp.concatenate([vv, ones], axis=1)
            pv = jnp.dot(p_all[b], aug, preferred_element_type=F32)
            rows = dst(b)
            acc_o[rows, :] = pv[:, :BLK]
            den_o[rows, :] = pv[:, BLK:]
            max_o[rows, :] = jnp.broadcast_to(m_all[b], (BLK, BLK))

    ngroups = seq // BLK // ATTN_GROUP

    def banded(qs, ks, vs, outs, blocks_per_segment, dst_of=None, between=None):
        span = ATTN_GROUP * BLK
        if dst_of is None:
            dst_of = lambda g, base: (lambda b: pl.ds(base + b * BLK, BLK))
        for g in range(ngroups):
            if blocks_per_segment == ngroups * ATTN_GROUP:
                has_prev = [g > 0] + [True] * (ATTN_GROUP - 1)
            else:
                assert ATTN_GROUP % blocks_per_segment == 0
                has_prev = [b % blocks_per_segment != 0 for b in range(ATTN_GROUP)]
            banded_group(g * span, qs, ks, vs, outs, has_prev, dst_of(g, g * span))
            if between is not None:
                between(g)

    seq4 = seq // 4
    seq16 = seq // 16

    qf[...] = q_ref[...].astype(F32)
    kf[...] = k_ref[...].astype(F32)
    vf[...] = v_ref[...].astype(F32)

    def gather4(r):
        src = pl.ds(r, seq4, stride=4)
        dst = pl.ds(r * seq4, seq4)
        for f32_src, f32_dst, bf16_dst in ((qf, q4, qd), (kf, k4, kd), (vf, v4, vd)):
            x = f32_src[src, :]
            f32_dst[dst, :] = x
            bf16_dst[dst, :] = x.astype(BF16)

    def gather16(r):
        src = pl.ds((r % 4) * seq4 + r // 4, seq16, stride=4)
        dst = pl.ds(r * seq16, seq16)
        qe[dst, :] = q4[src, :].astype(BF16)
        ke[dst, :] = k4[src, :].astype(BF16)
        ve[dst, :] = v4[src, :].astype(BF16)

    def gather4_piece(g):
        for r in range(g * 4 // ngroups, (g + 1) * 4 // ngroups):
            gather4(r)

    def gather16_piece(g):
        for r in range(g * 16 // ngroups, (g + 1) * 16 // ngroups):
            gather16(r)

    banded(q_ref, k_ref, v_ref, (a1, m1, d1), seq // BLK, between=gather4_piece)
    banded(qd, kd, vd, (a4, m4, d4), seq4 // BLK, between=gather16_piece)


    blocks16 = seq16 // BLK
    a16, m16, d16 = qf, kf, vf

    def dst16(g, base):
        return lambda b: pl.ds((b // blocks16) * seq4 + g + 4 * BLK * (b % blocks16), BLK, stride=4)

    banded(qe, ke, ve, (a16, m16, d16), blocks16, dst16)

    out = q4
    chunk = 256

    for i in range(seq // chunk):
        r = i // (seq4 // chunk)
        j = i % (seq4 // chunk)
        nat = pl.ds(r + 4 * chunk * j, chunk, stride=4)
        loc = pl.ds(i * chunk, chunk)
        x1, x4, x16 = m1[nat, :], m4[loc, :], m16[loc, :]
        top = jnp.maximum(jnp.maximum(x1, x4), x16)
        e1, e4, e16 = jnp.exp(x1 - top), jnp.exp(x4 - top), jnp.exp(x16 - top)
        num = e1 * a1[nat, :] + e4 * a4[loc, :] + e16 * a16[loc, :]
        den = e1 * d1[nat, :] + e4 * d4[loc, :] + e16 * d16[loc, :]
        out[nat, :] = num / den
    o_ref[...] = out[...].astype(BF16)


def _attention(proj3, cast_weights, layer):
    B, S, _ = proj3.shape
    H = ATTN_HEADS
    blk = lambda off: pl.BlockSpec((None, S, HEAD_DIM), lambda b, h, off=off: (b, 0, off + h))
    f32buf = pltpu.VMEM((S, HEAD_DIM), F32)
    bf16buf = pltpu.VMEM((S, HEAD_DIM), BF16)
    job = _CastJob(cast_weights, layer, B * H, lambda b, h: b * H + h)
    outs = pl.pallas_call(
        functools.partial(_attn_kernel, seq=S, n_cast=job.n),
        grid=(B, H),
        in_specs=[blk(0), blk(H), blk(2 * H)] + job.in_specs,
        out_specs=[pl.BlockSpec((None, S, HEAD_DIM), lambda b, h: (b, 0, h))] + job.out_specs,
        out_shape=[jax.ShapeDtypeStruct((B, S, ATTN_WIDTH), BF16)] + job.out_shape,
        scratch_shapes=[f32buf] * 6 + [bf16buf] * 6 + [f32buf] * 6,
        compiler_params=_params("arbitrary", "arbitrary"),
        name="dilated_attention",
    )(proj3, proj3, proj3, *cast_weights)
    return outs[0], outs[1:]


def _ssm_prep_kernel(lr_ref, li_ref, ldt_ref, br_ref, bi_ref, ar_ref, ai_ref, bbr_ref, bbi_ref):
    lr = lr_ref[...]
    li = li_ref[...]
    dt = jnp.exp(ldt_ref[...])
    mag = jnp.exp(lr * dt)
    abar_r = mag * jnp.cos(li * dt)
    abar_i = mag * jnp.sin(li * dt)
    num_r = abar_r - 1.0
    num_i = abar_i
    den = lr * lr + li * li
    coef_r = (num_r * lr + num_i * li) / den
    coef_i = (num_i * lr - num_r * li) / den
    br = br_ref[...]
    bi = bi_ref[...]
    ar_ref[...] = abar_r
    ai_ref[...] = -abar_i
    bbr_ref[...] = coef_r * br - coef_i * bi
    bbi_ref[...] = -(coef_r * bi + coef_i * br)


def _ssm_prep(lam_re, lam_im, log_dt, b_re, b_im):
    G, P, C = SSM_GROUPS, SSM_STATE, SSM_GROUP
    gp = jax.ShapeDtypeStruct((G, 1, P), F32)
    gcp = jax.ShapeDtypeStruct((G, C, P), F32)
    return pl.pallas_call(
        _ssm_prep_kernel,
        out_shape=(gp, gp, gcp, gcp),
        name="ssm_prep",
    )(lam_re.reshape(G, 1, P), lam_im.reshape(G, 1, P), log_dt.reshape(G, 1, 1),
      jnp.swapaxes(b_re, 1, 2), jnp.swapaxes(b_im, 1, 2))


def _block_diag_slabs(m):
    G, a, b = m.shape
    gl = SSM_LANE_BLOCK // SSM_GROUP
    m4 = m.reshape(SSM_SLABS, gl, a, b)
    eye = jnp.eye(gl, dtype=m.dtype)
    out = m4[:, :, :, None, :] * eye[None, :, None, :, None]
    return out.reshape(SSM_SLABS, gl * a, gl * b)


def _chunk_permutation():
    i = np.arange(SSM_TILE)
    j = (i % SSM_CHUNK_LEN) * SSM_CHUNKS + i // SSM_CHUNK_LEN
    perm = np.zeros((SSM_TILE, SSM_TILE), np.float32)
    perm[j, i] = 1.0
    return perm


def _ssm_kernel(*refs, n_cast):
    n_in = 10
    cast_in, cast_out = refs[n_in:n_in + n_cast], refs[n_in + n_cast + 1:n_in + 2 * n_cast + 1]
    _CastJob.run(cast_in, cast_out)
    _ssm_body(*refs[:n_in], refs[n_in + n_cast], *refs[n_in + 2 * n_cast + 1:])


def _ssm_body(u_ref, perm_ref, permt_ref, a_ref, bmat_ref, cmat_ref, d_ref, wglu_ref, bglu_ref,
              gn_ref, o_ref, up_s, bu_s, y_s, car_s):
    L = SSM_CHUNK_LEN
    NS = SLAB_STATES

    @pl.when(pl.program_id(1) == 0)
    def _():
        car_s[...] = jnp.zeros_like(car_s)

    up_s[...] = jnp.dot(perm_ref[...], u_ref[...], preferred_element_type=F32).astype(BF16)
    sub = lax.broadcasted_iota(jnp.int32, (SSM_CHUNKS, NS), 0)

    def lanes_of(kb):
        return slice(kb * SSM_LANE_BLOCK, (kb + 1) * SSM_LANE_BLOCK)

    nbuf = bu_s.shape[0]
    zero = jnp.zeros((SSM_CHUNKS, NS), F32)

    def project_in(kb):
        bu_s[kb % nbuf] = jnp.dot(up_s[:, lanes_of(kb)], bmat_ref[kb], preferred_element_type=F32)

    def scan(slabs, init, store):
        state = list(init)
        for t in range(L):
            rows = slice(t * SSM_CHUNKS, (t + 1) * SSM_CHUNKS)
            for k, kb in enumerate(slabs):
                buf = bu_s.at[kb % nbuf]
                a_r, a_i = coef[kb]
                s_r, s_i = state[k]
                s_r, s_i = (a_r * s_r - a_i * s_i + buf[rows, 0:NS],
                            a_r * s_i + a_i * s_r + buf[rows, NS:2 * NS])
                if store:
                    buf[rows, 0:NS] = s_r
                    buf[rows, NS:2 * NS] = s_i
                state[k] = (s_r, s_i)
        return state

    def chunk_inits(kb, final):
        f_r, f_i = final
        p_r, p_i = a_ref[kb, 0:1, :], a_ref[kb, 1:2, :]
        for _ in range(int(math.log2(L))):
            p_r, p_i = p_r * p_r - p_i * p_i, 2.0 * p_r * p_i
        g_r, g_i = car_s[kb:kb + 1, 0:NS], car_s[kb:kb + 1, NS:2 * NS]
        i_r, i_i = zero, zero
        for c in range(SSM_CHUNKS):
            i_r = jnp.where(sub == c, jnp.broadcast_to(g_r, (SSM_CHUNKS, NS)), i_r)
            i_i = jnp.where(sub == c, jnp.broadcast_to(g_i, (SSM_CHUNKS, NS)), i_i)
            g_r, g_i = (f_r[c:c + 1] + p_r * g_r - p_i * g_i,
                        f_i[c:c + 1] + p_r * g_i + p_i * g_r)
        car_s[kb:kb + 1, 0:NS] = g_r
        car_s[kb:kb + 1, NS:2 * NS] = g_i
        return i_r, i_i

    coef = {}
    together = nbuf // 2
    groups = [list(range(g, g + together)) for g in range(0, SSM_SLABS, together)]
    for kb in groups[0]:
        project_in(kb)
    for gi, slabs in enumerate(groups):
        if gi + 1 < len(groups):
            for kb in groups[gi + 1]:
                project_in(kb)
        for kb in slabs:
            coef[kb] = (jnp.broadcast_to(a_ref[kb, 0:1, :], (SSM_CHUNKS, NS)),
                        jnp.broadcast_to(a_ref[kb, 1:2, :], (SSM_CHUNKS, NS)))
        finals = scan(slabs, [(zero, zero)] * len(slabs), False)
        scan(slabs, [chunk_inits(kb, f) for kb, f in zip(slabs, finals)], True)
        for kb in slabs:
            y_s[:, lanes_of(kb)] = jnp.dot(bu_s[kb % nbuf].astype(BF16), cmat_ref[kb],
                                           preferred_element_type=F32)

    y = y_s[...] + d_ref[...] * up_s[...].astype(F32)
    g = 0.5 * y * (1.0 + jnp.tanh(math.sqrt(2.0 / math.pi) * (y + 0.044715 * (y * y * y))))
    gate = jax.nn.sigmoid(jnp.dot(g.astype(BF16), wglu_ref[...], preferred_element_type=F32)
                          + bglu_ref[...])
    on = _rms(g * gate, gn_ref[...]).astype(BF16)
    o_ref[...] = jnp.dot(permt_ref[...], on, preferred_element_type=F32).astype(BF16)


def _ssm(proj3, a_pack, bmat, cmat, d_skip, wglu, bglu, gnorm, cast_weights, layer):
    B, S, _ = proj3.shape
    ucol = 3 * ATTN_WIDTH // SSM_WIDTH
    perm = _chunk_permutation()
    const = lambda shape: pl.BlockSpec(shape, lambda b, i: (0,) * len(shape))
    tiles = S // SSM_TILE
    job = _CastJob(cast_weights, layer, B * tiles, lambda b, i: b * tiles + i)
    outs = pl.pallas_call(
        functools.partial(_ssm_kernel, n_cast=job.n),
        grid=(B, tiles),
        in_specs=[
            pl.BlockSpec((None, SSM_TILE, SSM_WIDTH), lambda b, i: (b, i, ucol)),
            const((SSM_TILE, SSM_TILE)),
            const((SSM_TILE, SSM_TILE)),
            const((SSM_SLABS, 2, SLAB_STATES)),
            const((SSM_SLABS, SSM_LANE_BLOCK, 2 * SLAB_STATES)),
            const((SSM_SLABS, 2 * SLAB_STATES, SSM_LANE_BLOCK)),
            const((1, SSM_WIDTH)),
            const((SSM_WIDTH, SSM_WIDTH)),
            const((1, SSM_WIDTH)),
            const((1, SSM_WIDTH)),
        ] + job.in_specs,
        out_specs=[pl.BlockSpec((None, SSM_TILE, SSM_WIDTH), lambda b, i: (b, i, 0))] + job.out_specs,
        out_shape=[jax.ShapeDtypeStruct((B, S, SSM_WIDTH), BF16)] + job.out_shape,
        scratch_shapes=[
            pltpu.VMEM((SSM_TILE, SSM_WIDTH), BF16),
            pltpu.VMEM((4, SSM_TILE, 2 * SLAB_STATES), F32),
            pltpu.VMEM((SSM_TILE, SSM_WIDTH), F32),
            pltpu.VMEM((SSM_SLABS, 2 * SLAB_STATES), F32),
        ],
        compiler_params=_params("arbitrary", "arbitrary"),
        name="s5_scan",
    )(proj3, jnp.asarray(perm, BF16), jnp.asarray(perm.T, BF16), a_pack, bmat, cmat,
      d_skip, wglu, bglu, gnorm, *cast_weights)
    return outs[0], outs[1:]


def _mix_kernel(x_ref, attn_ref, ssm_ref, ga_ref, w_ref, gp_ref, o_ref):
    an = _rms(attn_ref[...].astype(F32), ga_ref[...]).astype(BF16)
    mixed = jnp.concatenate([an, ssm_ref[...]], axis=1)
    y = jnp.dot(mixed, w_ref[...], preferred_element_type=F32)
    o_ref[...] = x_ref[...] + _rms(y, gp_ref[...])


def _mix(x2, attn2, ssm2, g_attn, w_out, g_post, tm=512):
    T, D = x2.shape
    row = lambda w: pl.BlockSpec((tm, w), lambda i: (i, 0))
    const = lambda shape: pl.BlockSpec(shape, lambda i: (0, 0))
    return pl.pallas_call(
        _mix_kernel,
        grid=(T // tm,),
        in_specs=[row(D), row(ATTN_WIDTH), row(SSM_WIDTH), const((1, ATTN_WIDTH)),
                  const((D, D)), const((1, D))],
        out_specs=row(D),
        out_shape=jax.ShapeDtypeStruct((T, D), F32),
        compiler_params=_params("arbitrary"),
        name="mix_out",
    )(x2, attn2, ssm2, g_attn, w_out, g_post)


def _mlp_kernel(h_ref, gpre_ref, wup_ref, wdn_ref, gpost_ref, o_ref, hn_ref, *, sub):
    j = pl.program_id(1)

    @pl.when(j == 0)
    def _():
        hn_ref[...] = _rms(h_ref[...], gpre_ref[...]).astype(BF16)
        o_ref[...] = jnp.zeros_like(o_ref)

    part = None
    for c in range(wup_ref.shape[1] // sub):
        cols = slice(c * sub, (c + 1) * sub)
        a = jnp.dot(hn_ref[...], wup_ref[:, cols], preferred_element_type=F32)
        a = jnp.square(jnp.maximum(a, 0.0)).astype(BF16)
        d = jnp.dot(a, wdn_ref[cols, :], preferred_element_type=F32)
        part = d if part is None else part + d
    o_ref[...] += part

    @pl.when(j == pl.num_programs(1) - 1)
    def _():
        o_ref[...] = h_ref[...] + _rms(o_ref[...], gpost_ref[...])


def _mlp(h, g_pre, w_up, w_down, g_post, tm=512, tf=2048, sub=1024):
    T, D = h.shape
    F = w_up.shape[1]
    return pl.pallas_call(
        functools.partial(_mlp_kernel, sub=sub),
        grid=(T // tm, F // tf),
        in_specs=[
            pl.BlockSpec((tm, D), lambda i, j: (i, 0)),
            pl.BlockSpec((1, D), lambda i, j: (0, 0)),
            pl.BlockSpec((D, tf), lambda i, j: (0, j)),
            pl.BlockSpec((tf, D), lambda i, j: (j, 0)),
            pl.BlockSpec((1, D), lambda i, j: (0, 0)),
        ],
        out_specs=pl.BlockSpec((tm, D), lambda i, j: (i, 0)),
        out_shape=jax.ShapeDtypeStruct((T, D), F32),
        scratch_shapes=[pltpu.VMEM((tm, D), BF16)],
        compiler_params=_params("arbitrary", "arbitrary", vmem_limit=MLP_VMEM_LIMIT),
        name="relu2_mlp",
    )(h, g_pre, w_up, w_down, g_post)


def _ple_kernel(h_ref, p_ref, gpre_ref, wg_ref, wp_ref, gpost_ref, o_ref):
    h = h_ref[...]
    hn = _rms(h, gpre_ref[...]).astype(BF16)
    gate = jax.nn.sigmoid(jnp.dot(hn, wg_ref[...], preferred_element_type=F32))
    e = jnp.dot(p_ref[...].astype(BF16), wp_ref[...], preferred_element_type=F32)
    o_ref[...] = h + _rms(gate * e, gpost_ref[...])


def _ple(h, p2, g_pre, w_gate, w_proj, g_post, tm=512):
    T, D = h.shape
    row = lambda w: pl.BlockSpec((tm, w), lambda i: (i, 0))
    const = lambda shape: pl.BlockSpec(shape, lambda i: (0, 0))
    return pl.pallas_call(
        _ple_kernel,
        grid=(T // tm,),
        in_specs=[row(D), row(PLE_DIM), const((1, D)), const((D, D)), const((PLE_DIM, D)),
                  const((1, D))],
        out_specs=row(D),
        out_shape=jax.ShapeDtypeStruct((T, D), F32),
        compiler_params=_params("arbitrary"),
        name="ple_gate",
    )(h, p2, g_pre, w_gate, w_proj, g_post)


def kernel(x, p, mix_norm_pre, w_in, lam_re, lam_im, log_dt, ssm_b_re, ssm_b_im, ssm_c_re, ssm_c_im, ssm_d, w_glu, b_glu, attn_out_norm, ssm_out_norm, w_out, mix_norm_post, mlp_norm_pre, w_up, w_down, mlp_norm_post, ple_norm_pre, w_ple_gate, w_ple_proj, ple_norm_post):
    B, S, D = x.shape
    T = B * S
    h = x.reshape(T, D)
    for i in range(p.shape[0]):
        vec = lambda a: a[i].reshape(1, -1)
        proj = _in_proj(h, vec(mix_norm_pre), w_in[i].astype(BF16))
        proj3 = proj.reshape(B, S, -1)
        attn, (w_out_b, w_gate_b) = _attention(proj3, [w_out, w_ple_gate], i)

        a_r, a_i, bb_r, bb_i = _ssm_prep(lam_re[i], lam_im[i], log_dt[i], ssm_b_re[i], ssm_b_im[i])
        a_pack = jnp.stack([a_r.reshape(SSM_SLABS, SLAB_STATES), a_i.reshape(SSM_SLABS, SLAB_STATES)],
                           axis=1)
        bmat = jnp.concatenate([_block_diag_slabs(bb_r), _block_diag_slabs(bb_i)], axis=2).astype(BF16)
        cmat = jnp.concatenate([_block_diag_slabs(jnp.swapaxes(ssm_c_re[i], 1, 2)),
                                _block_diag_slabs(jnp.swapaxes(ssm_c_im[i], 1, 2))], axis=1).astype(BF16)
        ssm, (w_up_b, w_down_b) = _ssm(proj3, a_pack, bmat, cmat, vec(ssm_d), w_glu[i].astype(BF16),
                                       vec(b_glu), vec(ssm_out_norm), [w_up, w_down], i)

        h = _mix(h, attn.reshape(T, -1), ssm.reshape(T, -1), vec(attn_out_norm),
                 w_out_b, vec(mix_norm_post))
        h = _mlp(h, vec(mlp_norm_pre), w_up_b, w_down_b, vec(mlp_norm_post))
        h = _ple(h, p[i].reshape(T, -1), vec(ple_norm_pre), w_gate_b,
                 w_ple_proj[i].astype(BF16), vec(ple_norm_post))
    return h.reshape(B, S, D)
```

```python
import functools
import math

import numpy as np
import jax
import jax.numpy as jnp
from jax import lax
from jax.experimental import pallas as pl
from jax.experimental.pallas import tpu as pltpu

F32 = jnp.float32
BF16 = jnp.bfloat16

D_MODEL = 2048
HEAD_DIM = 128
ATTN_WIDTH = 1024
ATTN_HEADS = ATTN_WIDTH // HEAD_DIM
SSM_WIDTH = 1024
SSM_GROUP = 16
SSM_GROUPS = SSM_WIDTH // SSM_GROUP
SSM_STATE = 64
DILATIONS = (1, 4, 16)
BLK = 128
ATTN_GROUP = 8
D_FF = 4 * D_MODEL
PLE_DIM = 256
RMS_EPS = 1e-6
NEG_INF = -1e30

V7X_VMEM_BYTES = 64 * 1024 * 1024
VMEM_LIMIT = 52 * 1024 * 1024
MLP_VMEM_LIMIT = 58 * 1024 * 1024

SSM_TILE = 512
SSM_CHUNKS = 8
SSM_CHUNK_LEN = SSM_TILE // SSM_CHUNKS
SSM_LANE_BLOCK = 128
SSM_SLABS = SSM_WIDTH // SSM_LANE_BLOCK
SLAB_STATES = (SSM_LANE_BLOCK // SSM_GROUP) * SSM_STATE


def _rms(x, g):
    ms = jnp.mean(x * x, axis=-1, keepdims=True)
    return x * lax.rsqrt(ms + RMS_EPS) * g


def _params(*sem, vmem_limit=VMEM_LIMIT):
    return pltpu.CompilerParams(dimension_semantics=sem, vmem_limit_bytes=vmem_limit)


class _CastJob:
    def __init__(self, weights, layer, steps, step_of):
        self.weights = weights
        self.n = len(weights)
        self.in_specs, self.out_specs, self.out_shape = [], [], []
        for w in weights:
            _, rows, cols = w.shape
            assert rows % steps == 0
            blk = rows // steps
            self.in_specs.append(pl.BlockSpec((None, blk, cols),
                                              lambda *g, layer=layer: (layer, step_of(*g), 0)))
            self.out_specs.append(pl.BlockSpec((blk, cols), lambda *g: (step_of(*g), 0)))
            self.out_shape.append(jax.ShapeDtypeStruct((rows, cols), BF16))

    @staticmethod
    def run(srcs, dsts):
        for src, dst in zip(srcs, dsts):
            dst[...] = src[...].astype(BF16)


def _in_proj_kernel(x_ref, g_ref, w_ref, o_ref, xn_ref):
    j = pl.program_id(1)
    tn = o_ref.shape[1]

    @pl.when(j == 0)
    def _():
        xn_ref[...] = _rms(x_ref[...], g_ref[...]).astype(BF16)

    col = j * tn + lax.broadcasted_iota(jnp.int32, (1, tn), 1)
    col_scale = jnp.where(col < ATTN_WIDTH, 1.0 / math.sqrt(HEAD_DIM), 1.0)
    acc = jnp.dot(xn_ref[...], w_ref[...], preferred_element_type=F32)
    o_ref[...] = (acc * col_scale).astype(BF16)


def _in_proj(x2, g, w_bf16, tm=1024, tn=2 * ATTN_WIDTH):
    T, D = x2.shape
    N = w_bf16.shape[1]
    return pl.pallas_call(
        _in_proj_kernel,
        grid=(T // tm, N // tn),
        in_specs=[
            pl.BlockSpec((tm, D), lambda i, j: (i, 0)),
            pl.BlockSpec((1, D), lambda i, j: (0, 0)),
            pl.BlockSpec((D, tn), lambda i, j: (0, j)),
        ],
        out_specs=pl.BlockSpec((tm, tn), lambda i, j: (i, j)),
        out_shape=jax.ShapeDtypeStruct((T, N), BF16),
        scratch_shapes=[pltpu.VMEM((tm, D), BF16)],
        compiler_params=_params("arbitrary", "arbitrary"),
        name="in_proj",
    )(x2, g, w_bf16)


def _attn_kernel(q_ref, k_ref, v_ref, *refs, seq, n_cast):
    cast_in, o_ref, cast_out = refs[:n_cast], refs[n_cast], refs[n_cast + 1:2 * n_cast + 1]
    _CastJob.run(cast_in, cast_out)
    _attn_body(q_ref, k_ref, v_ref, o_ref, *refs[2 * n_cast + 1:], seq=seq)


def _attn_body(q_ref, k_ref, v_ref, o_ref, qf, kf, vf, q4, k4, v4, qd, kd, vd, qe, ke, ve,
               a1, m1, d1, a4, m4, d4, *, seq):
    row = lax.broadcasted_iota(jnp.int32, (BLK, BLK), 0)
    col = lax.broadcasted_iota(jnp.int32, (BLK, BLK), 1)
    cur_ok = col <= row
    both_ok = jnp.concatenate([col >= row, cur_ok], axis=1)
    ones = jnp.ones((BLK, BLK), BF16)

    def banded_group(base, qs, ks, vs, outs, has_prev, dst):
        acc_o, max_o, den_o = outs
        nb = len(has_prev)
        row0 = [base + b * BLK for b in range(nb)]
        s_all, m_all = [], []
        for b in range(nb):
            q = qs[pl.ds(row0[b], BLK), :]
            if has_prev[b]:
                kk = ks[pl.ds(row0[b] - BLK, 2 * BLK), :]
                mask = both_ok
            else:
                kk = ks[pl.ds(row0[b], BLK), :]
                mask = cur_ok
            s = lax.dot_general(q, kk, (((1,), (1,)), ((), ())), preferred_element_type=F32)
            s = jnp.where(mask, s, NEG_INF)
            if has_prev[b]:
                m = jnp.max(jnp.maximum(s[:, :BLK], s[:, BLK:]), axis=1, keepdims=True)
            else:
                m = jnp.max(s, axis=1, keepdims=True)
            s_all.append(s)
            m_all.append(m)
        p_all = [jnp.exp(s - m).astype(BF16) for s, m in zip(s_all, m_all)]
        for b in range(nb):
            if has_prev[b]:
                vv = vs[pl.ds(row0[b] - BLK, 2 * BLK), :]
                aug = jnp.concatenate([vv, jnp.concatenate([ones, ones], axis=0)], axis=1)
            else:
                vv = vs[pl.ds(row0[b], BLK), :]
                aug = jnp.concatenate([vv, ones], axis=1)
            pv = jnp.dot(p_all[b], aug, preferred_element_type=F32)
            rows = dst(b)
            acc_o[rows, :] = pv[:, :BLK]
            den_o[rows, :] = pv[:, BLK:]
            max_o[rows, :] = jnp.broadcast_to(m_all[b], (BLK, BLK))

    ngroups = seq // BLK // ATTN_GROUP

    def banded(qs, ks, vs, outs, blocks_per_segment, dst_of=None, between=None):
        span = ATTN_GROUP * BLK
        if dst_of is None:
            dst_of = lambda g, base: (lambda b: pl.ds(base + b * BLK, BLK))
        for g in range(ngroups):
            if blocks_per_segment == ngroups * ATTN_GROUP:
                has_prev = [g > 0] + [True] * (ATTN_GROUP - 1)
            else:
                assert ATTN_GROUP % blocks_per_segment == 0
                has_prev = [b % blocks_per_segment != 0 for b in range(ATTN_GROUP)]
            banded_group(g * span, qs, ks, vs, outs, has_prev, dst_of(g, g * span))
            if between is not None:
                between(g)

    seq4 = seq // 4
    seq16 = seq // 16

    qf[...] = q_ref[...].astype(F32)
    kf[...] = k_ref[...].astype(F32)
    vf[...] = v_ref[...].astype(F32)

    def gather4(r):
        src = pl.ds(r, seq4, stride=4)
        dst = pl.ds(r * seq4, seq4)
        for f32_src, f32_dst, bf16_dst in ((qf, q4, qd), (kf, k4, kd), (vf, v4, vd)):
            x = f32_src[src, :]
            f32_dst[dst, :] = x
            bf16_dst[dst, :] = x.astype(BF16)

    def gather16(r):
        src = pl.ds((r % 4) * seq4 + r // 4, seq16, stride=4)
        dst = pl.ds(r * seq16, seq16)
        qe[dst, :] = q4[src, :].astype(BF16)
        ke[dst, :] = k4[src, :].astype(BF16)
        ve[dst, :] = v4[src, :].astype(BF16)

    def gather4_piece(g):
        for r in range(g * 4 // ngroups, (g + 1) * 4 // ngroups):
            gather4(r)

    def gather16_piece(g):
        for r in range(g * 16 // ngroups, (g + 1) * 16 // ngroups):
            gather16(r)

    banded(q_ref, k_ref, v_ref, (a1, m1, d1), seq // BLK, between=gather4_piece)
    banded(qd, kd, vd, (a4, m4, d4), seq4 // BLK, between=gather16_piece)


    blocks16 = seq16 // BLK
    a16, m16, d16 = qf, kf, vf

    def dst16(g, base):
        return lambda b: pl.ds((b // blocks16) * seq4 + g + 4 * BLK * (b % blocks16), BLK, stride=4)

    banded(qe, ke, ve, (a16, m16, d16), blocks16, dst16)

    out = q4
    chunk = 256

    for i in range(seq // chunk):
        r = i // (seq4 // chunk)
        j = i % (seq4 // chunk)
        nat = pl.ds(r + 4 * chunk * j, chunk, stride=4)
        loc = pl.ds(i * chunk, chunk)
        x1, x4, x16 = m1[nat, :], m4[loc, :], m16[loc, :]
        top = jnp.maximum(jnp.maximum(x1, x4), x16)
        e1, e4, e16 = jnp.exp(x1 - top), jnp.exp(x4 - top), jnp.exp(x16 - top)
        num = e1 * a1[nat, :] + e4 * a4[loc, :] + e16 * a16[loc, :]
        den = e1 * d1[nat, :] + e4 * d4[loc, :] + e16 * d16[loc, :]
        out[nat, :] = num / den
    o_ref[...] = out[...].astype(BF16)


def _attention(proj3, cast_weights, layer):
    B, S, _ = proj3.shape
    H = ATTN_HEADS
    blk = lambda off: pl.BlockSpec((None, S, HEAD_DIM), lambda b, h, off=off: (b, 0, off + h))
    f32buf = pltpu.VMEM((S, HEAD_DIM), F32)
    bf16buf = pltpu.VMEM((S, HEAD_DIM), BF16)
    job = _CastJob(cast_weights, layer, B * H, lambda b, h: b * H + h)
    outs = pl.pallas_call(
        functools.partial(_attn_kernel, seq=S, n_cast=job.n),
        grid=(B, H),
        in_specs=[blk(0), blk(H), blk(2 * H)] + job.in_specs,
        out_specs=[pl.BlockSpec((None, S, HEAD_DIM), lambda b, h: (b, 0, h))] + job.out_specs,
        out_shape=[jax.ShapeDtypeStruct((B, S, ATTN_WIDTH), BF16)] + job.out_shape,
        scratch_shapes=[f32buf] * 6 + [bf16buf] * 6 + [f32buf] * 6,
        compiler_params=_params("arbitrary", "arbitrary"),
        name="dilated_attention",
    )(proj3, proj3, proj3, *cast_weights)
    return outs[0], outs[1:]


def _ssm_prep_kernel(lr_ref, li_ref, ldt_ref, br_ref, bi_ref, ar_ref, ai_ref, bbr_ref, bbi_ref):
    lr = lr_ref[...]
    li = li_ref[...]
    dt = jnp.exp(ldt_ref[...])
    mag = jnp.exp(lr * dt)
    abar_r = mag * jnp.cos(li * dt)
    abar_i = mag * jnp.sin(li * dt)
    num_r = abar_r - 1.0
    num_i = abar_i
    den = lr * lr + li * li
    coef_r = (num_r * lr + num_i * li) / den
    coef_i = (num_i * lr - num_r * li) / den
    br = br_ref[...]
    bi = bi_ref[...]
    ar_ref[...] = abar_r
    ai_ref[...] = -abar_i
    bbr_ref[...] = coef_r * br - coef_i * bi
    bbi_ref[...] = -(coef_r * bi + coef_i * br)


def _ssm_prep(lam_re, lam_im, log_dt, b_re, b_im):
    G, P, C = SSM_GROUPS, SSM_STATE, SSM_GROUP
    gp = jax.ShapeDtypeStruct((G, 1, P), F32)
    gcp = jax.ShapeDtypeStruct((G, C, P), F32)
    return pl.pallas_call(
        _ssm_prep_kernel,
        out_shape=(gp, gp, gcp, gcp),
        name="ssm_prep",
    )(lam_re.reshape(G, 1, P), lam_im.reshape(G, 1, P), log_dt.reshape(G, 1, 1),
      jnp.swapaxes(b_re, 1, 2), jnp.swapaxes(b_im, 1, 2))


def _block_diag_slabs(m):
    G, a, b = m.shape
    gl = SSM_LANE_BLOCK // SSM_GROUP
    m4 = m.reshape(SSM_SLABS, gl, a, b)
    eye = jnp.eye(gl, dtype=m.dtype)
    out = m4[:, :, :, None, :] * eye[None, :, None, :, None]
    return out.reshape(SSM_SLABS, gl * a, gl * b)


def _chunk_permutation():
    i = np.arange(SSM_TILE)
    j = (i % SSM_CHUNK_LEN) * SSM_CHUNKS + i // SSM_CHUNK_LEN
    perm = np.zeros((SSM_TILE, SSM_TILE), np.float32)
    perm[j, i] = 1.0
    return perm


def _ssm_kernel(*refs, n_cast):
    n_in = 10
    cast_in, cast_out = refs[n_in:n_in + n_cast], refs[n_in + n_cast + 1:n_in + 2 * n_cast + 1]
    _CastJob.run(cast_in, cast_out)
    _ssm_body(*refs[:n_in], refs[n_in + n_cast], *refs[n_in + 2 * n_cast + 1:])


def _ssm_body(u_ref, perm_ref, permt_ref, a_ref, bmat_ref, cmat_ref, d_ref, wglu_ref, bglu_ref,
              gn_ref, o_ref, up_s, bu_s, y_s, car_s):
    L = SSM_CHUNK_LEN
    NS = SLAB_STATES

    @pl.when(pl.program_id(1) == 0)
    def _():
        car_s[...] = jnp.zeros_like(car_s)

    up_s[...] = jnp.dot(perm_ref[...], u_ref[...], preferred_element_type=F32).astype(BF16)
    sub = lax.broadcasted_iota(jnp.int32, (SSM_CHUNKS, NS), 0)

    def lanes_of(kb):
        return slice(kb * SSM_LANE_BLOCK, (kb + 1) * SSM_LANE_BLOCK)

    nbuf = bu_s.shape[0]
    zero = jnp.zeros((SSM_CHUNKS, NS), F32)

    def project_in(kb):
        bu_s[kb % nbuf] = jnp.dot(up_s[:, lanes_of(kb)], bmat_ref[kb], preferred_element_type=F32)

    def scan(slabs, init, store):
        state = list(init)
        for t in range(L):
            rows = slice(t * SSM_CHUNKS, (t + 1) * SSM_CHUNKS)
            for k, kb in enumerate(slabs):
                buf = bu_s.at[kb % nbuf]
                a_r, a_i = coef[kb]
                s_r, s_i = state[k]
                s_r, s_i = (a_r * s_r - a_i * s_i + buf[rows, 0:NS],
                            a_r * s_i + a_i * s_r + buf[rows, NS:2 * NS])
                if store:
                    buf[rows, 0:NS] = s_r
                    buf[rows, NS:2 * NS] = s_i
                state[k] = (s_r, s_i)
        return state

    def chunk_inits(kb, final):
        f_r, f_i = final
        p_r, p_i = a_ref[kb, 0:1, :], a_ref[kb, 1:2, :]
        for _ in range(int(math.log2(L))):
            p_r, p_i = p_r * p_r - p_i * p_i, 2.0 * p_r * p_i
        g_r, g_i = car_s[kb:kb + 1, 0:NS], car_s[kb:kb + 1, NS:2 * NS]
        i_r, i_i = zero, zero
        for c in range(SSM_CHUNKS):
            i_r = jnp.where(sub == c, jnp.broadcast_to(g_r, (SSM_CHUNKS, NS)), i_r)
            i_i = jnp.where(sub == c, jnp.broadcast_to(g_i, (SSM_CHUNKS, NS)), i_i)
            g_r, g_i = (f_r[c:c + 1] + p_r * g_r - p_i * g_i,
                        f_i[c:c + 1] + p_r * g_i + p_i * g_r)
        car_s[kb:kb + 1, 0:NS] = g_r
        car_s[kb:kb + 1, NS:2 * NS] = g_i
        return i_r, i_i

    coef = {}
    together = nbuf // 2
    groups = [list(range(g, g + together)) for g in range(0, SSM_SLABS, together)]
    for kb in groups[0]:
        project_in(kb)
    for gi, slabs in enumerate(groups):
        if gi + 1 < len(groups):
            for kb in groups[gi + 1]:
                project_in(kb)
        for kb in slabs:
            coef[kb] = (jnp.broadcast_to(a_ref[kb, 0:1, :], (SSM_CHUNKS, NS)),
                        jnp.broadcast_to(a_ref[kb, 1:2, :], (SSM_CHUNKS, NS)))
        finals = scan(slabs, [(zero, zero)] * len(slabs), False)
        scan(slabs, [chunk_inits(kb, f) for kb, f in zip(slabs, finals)], True)
        for kb in slabs:
            y_s[:, lanes_of(kb)] = jnp.dot(bu_s[kb % nbuf].astype(BF16), cmat_ref[kb],
                                           preferred_element_type=F32)

    y = y_s[...] + d_ref[...] * up_s[...].astype(F32)
    g = 0.5 * y * (1.0 + jnp.tanh(math.sqrt(2.0 / math.pi) * (y + 0.044715 * (y * y * y))))
    gate = jax.nn.sigmoid(jnp.dot(g.astype(BF16), wglu_ref[...], preferred_element_type=F32)
                          + bglu_ref[...])
    on = _rms(g * gate, gn_ref[...]).astype(BF16)
    o_ref[...] = jnp.dot(permt_ref[...], on, preferred_element_type=F32).astype(BF16)


def _ssm(proj3, a_pack, bmat, cmat, d_skip, wglu, bglu, gnorm, cast_weights, layer):
    B, S, _ = proj3.shape
    ucol = 3 * ATTN_WIDTH // SSM_WIDTH
    perm = _chunk_permutation()
    const = lambda shape: pl.BlockSpec(shape, lambda b, i: (0,) * len(shape))
    tiles = S // SSM_TILE
    job = _CastJob(cast_weights, layer, B * tiles, lambda b, i: b * tiles + i)
    outs = pl.pallas_call(
        functools.partial(_ssm_kernel, n_cast=job.n),
        grid=(B, tiles),
        in_specs=[
            pl.BlockSpec((None, SSM_TILE, SSM_WIDTH), lambda b, i: (b, i, ucol)),
            const((SSM_TILE, SSM_TILE)),
            const((SSM_TILE, SSM_TILE)),
            const((SSM_SLABS, 2, SLAB_STATES)),
            const((SSM_SLABS, SSM_LANE_BLOCK, 2 * SLAB_STATES)),
            const((SSM_SLABS, 2 * SLAB_STATES, SSM_LANE_BLOCK)),
            const((1, SSM_WIDTH)),
            const((SSM_WIDTH, SSM_WIDTH)),
            const((1, SSM_WIDTH)),
            const((1, SSM_WIDTH)),
        ] + job.in_specs,
        out_specs=[pl.BlockSpec((None, SSM_TILE, SSM_WIDTH), lambda b, i: (b, i, 0))] + job.out_specs,
        out_shape=[jax.ShapeDtypeStruct((B, S, SSM_WIDTH), BF16)] + job.out_shape,
        scratch_shapes=[
            pltpu.VMEM((SSM_TILE, SSM_WIDTH), BF16),
            pltpu.VMEM((4, SSM_TILE, 2 * SLAB_STATES), F32),
            pltpu.VMEM((SSM_TILE, SSM_WIDTH), F32),
            pltpu.VMEM((SSM_SLABS, 2 * SLAB_STATES), F32),
        ],
        compiler_params=_params("arbitrary", "arbitrary"),
        name="s5_scan",
    )(proj3, jnp.asarray(perm, BF16), jnp.asarray(perm.T, BF16), a_pack, bmat, cmat,
      d_skip, wglu, bglu, gnorm, *cast_weights)
    return outs[0], outs[1:]


def _mix_kernel(x_ref, attn_ref, ssm_ref, ga_ref, w_ref, gp_ref, o_ref):
    an = _rms(attn_ref[...].astype(F32), ga_ref[...]).astype(BF16)
    y = (jnp.dot(an, w_ref[:ATTN_WIDTH, :], preferred_element_type=F32)
         + jnp.dot(ssm_ref[...], w_ref[ATTN_WIDTH:, :], preferred_element_type=F32))
    o_ref[...] = x_ref[...] + _rms(y, gp_ref[...])


def _mix(x2, attn2, ssm2, g_attn, w_out, g_post, tm=512):
    T, D = x2.shape
    row = lambda w: pl.BlockSpec((tm, w), lambda i: (i, 0))
    const = lambda shape: pl.BlockSpec(shape, lambda i: (0, 0))
    return pl.pallas_call(
        _mix_kernel,
        grid=(T // tm,),
        in_specs=[row(D), row(ATTN_WIDTH), row(SSM_WIDTH), const((1, ATTN_WIDTH)),
                  const((D, D)), const((1, D))],
        out_specs=row(D),
        out_shape=jax.ShapeDtypeStruct((T, D), F32),
        compiler_params=_params("arbitrary"),
        name="mix_out",
    )(x2, attn2, ssm2, g_attn, w_out, g_post)


def _mlp_kernel(h_ref, gpre_ref, wup_ref, wdn_ref, gpost_ref, o_ref, hn_ref, *, sub):
    j = pl.program_id(1)

    @pl.when(j == 0)
    def _():
        hn_ref[...] = _rms(h_ref[...], gpre_ref[...]).astype(BF16)
        o_ref[...] = jnp.zeros_like(o_ref)

    part = None
    for c in range(wup_ref.shape[1] // sub):
        cols = slice(c * sub, (c + 1) * sub)
        a = jnp.dot(hn_ref[...], wup_ref[:, cols], preferred_element_type=F32)
        a = jnp.square(jnp.maximum(a, 0.0)).astype(BF16)
        d = jnp.dot(a, wdn_ref[cols, :], preferred_element_type=F32)
        part = d if part is None else part + d
    o_ref[...] += part

    @pl.when(j == pl.num_programs(1) - 1)
    def _():
        o_ref[...] = h_ref[...] + _rms(o_ref[...], gpost_ref[...])


def _mlp(h, g_pre, w_up, w_down, g_post, tm=512, tf=2048, sub=1024):
    T, D = h.shape
    F = w_up.shape[1]
    return pl.pallas_call(
        functools.partial(_mlp_kernel, sub=sub),
        grid=(T // tm, F // tf),
        in_specs=[
            pl.BlockSpec((tm, D), lambda i, j: (i, 0)),
            pl.BlockSpec((1, D), lambda i, j: (0, 0)),
            pl.BlockSpec((D, tf), lambda i, j: (0, j)),
            pl.BlockSpec((tf, D), lambda i, j: (j, 0)),
            pl.BlockSpec((1, D), lambda i, j: (0, 0)),
        ],
        out_specs=pl.BlockSpec((tm, D), lambda i, j: (i, 0)),
        out_shape=jax.ShapeDtypeStruct((T, D), F32),
        scratch_shapes=[pltpu.VMEM((tm, D), BF16)],
        compiler_params=_params("arbitrary", "arbitrary", vmem_limit=MLP_VMEM_LIMIT),
        name="relu2_mlp",
    )(h, g_pre, w_up, w_down, g_post)


def _ple_kernel(h_ref, p_ref, gpre_ref, wg_ref, wp_ref, gpost_ref, o_ref):
    h = h_ref[...]
    hn = _rms(h, gpre_ref[...]).astype(BF16)
    gate = jax.nn.sigmoid(jnp.dot(hn, wg_ref[...], preferred_element_type=F32))
    e = jnp.dot(p_ref[...].astype(BF16), wp_ref[...], preferred_element_type=F32)
    o_ref[...] = h + _rms(gate * e, gpost_ref[...])


def _ple(h, p2, g_pre, w_gate, w_proj, g_post, tm=512):
    T, D = h.shape
    row = lambda w: pl.BlockSpec((tm, w), lambda i: (i, 0))
    const = lambda shape: pl.BlockSpec(shape, lambda i: (0, 0))
    return pl.pallas_call(
        _ple_kernel,
        grid=(T // tm,),
        in_specs=[row(D), row(PLE_DIM), const((1, D)), const((D, D)), const((PLE_DIM, D)),
                  const((1, D))],
        out_specs=row(D),
        out_shape=jax.ShapeDtypeStruct((T, D), F32),
        compiler_params=_params("arbitrary"),
        name="ple_gate",
    )(h, p2, g_pre, w_gate, w_proj, g_post)


def kernel(x, p, mix_norm_pre, w_in, lam_re, lam_im, log_dt, ssm_b_re, ssm_b_im, ssm_c_re, ssm_c_im, ssm_d, w_glu, b_glu, attn_out_norm, ssm_out_norm, w_out, mix_norm_post, mlp_norm_pre, w_up, w_down, mlp_norm_post, ple_norm_pre, w_ple_gate, w_ple_proj, ple_norm_post):
    B, S, D = x.shape
    T = B * S
    h = x.reshape(T, D)
    for i in range(p.shape[0]):
        vec = lambda a: a[i].reshape(1, -1)
        proj = _in_proj(h, vec(mix_norm_pre), w_in[i].astype(BF16))
        proj3 = proj.reshape(B, S, -1)
        attn, (w_out_b, w_gate_b) = _attention(proj3, [w_out, w_ple_gate], i)

        a_r, a_i, bb_r, bb_i = _ssm_prep(lam_re[i], lam_im[i], log_dt[i], ssm_b_re[i], ssm_b_im[i])
        a_pack = jnp.stack([a_r.reshape(SSM_SLABS, SLAB_STATES), a_i.reshape(SSM_SLABS, SLAB_STATES)],
                           axis=1)
        bmat = jnp.concatenate([_block_diag_slabs(bb_r), _block_diag_slabs(bb_i)], axis=2).astype(BF16)
        cmat = jnp.concatenate([_block_diag_slabs(jnp.swapaxes(ssm_c_re[i], 1, 2)),
                                _block_diag_slabs(jnp.swapaxes(ssm_c_im[i], 1, 2))], axis=1).astype(BF16)
        ssm, (w_up_b, w_down_b) = _ssm(proj3, a_pack, bmat, cmat, vec(ssm_d), w_glu[i].astype(BF16),
                                       vec(b_glu), vec(ssm_out_norm), [w_up, w_down], i)

        h = _mix(h, attn.reshape(T, -1), ssm.reshape(T, -1), vec(attn_out_norm),
                 w_out_b, vec(mix_norm_post))
        h = _mlp(h, vec(mlp_norm_pre), w_up_b, w_down_b, vec(mlp_norm_post))
        h = _ple(h, p[i].reshape(T, -1), vec(ple_norm_pre), w_gate_b,
                 w_ple_proj[i].astype(BF16), vec(ple_norm_post))
    return h.reshape(B, S, D)
```
